```python
import math
import jax, jax.numpy as jnp
from jax import lax
import numpy as np

D_MODEL = 1024
BATCH = 1
SEQ = 16384
DEPTH = 2
DEC_BATCH = 128
DEC_SEQ = 4
PAST_LEN = 16384
PAGE_SIZE = 128

N_MIXERS = 2
N_MLA_LAYERS = (DEPTH + 1) // 2
N_SSM_LAYERS = DEPTH // 2

N_HEADS = 8
D_NOPE = 128
D_ROPE = 64
D_V = 128
Q_LORA = 384
KV_LORA = 256
ROPE_THETA = 10000.0
Q_BLOCK = 128
ATTN_SCALE = 1.0 / math.sqrt(D_NOPE + D_ROPE)

GROUP_SIZE = 16
N_GROUPS = D_MODEL // GROUP_SIZE
STATE = 64
DT_MIN = 1e-3
DT_MAX = 1e-1

PEER_HEADS = 8
N_KEYS = 128
N_EXPERTS = N_KEYS * N_KEYS
D_KEY = 256
HALF_KEY = D_KEY // 2
PEER_TOPK = 16
PEER_BLOCK = 128

ALPHA = (2 * DEPTH) ** 0.25
BETA = (8 * DEPTH) ** -0.25
LN_EPS = 1e-5
RMS_EPS = 1e-6

kernel_name = "mla_s5_peer_deepnorm_step"

F32 = jnp.float32


def layer_norm(x, g, b):
    xf = x.astype(F32)
    mu = jnp.mean(xf, -1, keepdims=True)
    var = jnp.mean(jnp.square(xf - mu), -1, keepdims=True)
    return ((xf - mu) * lax.rsqrt(var + LN_EPS) * g.astype(F32) + b.astype(F32)).astype(x.dtype)


def rms_norm(x, g):
    xf = x.astype(F32)
    return (xf * lax.rsqrt(jnp.mean(xf * xf, -1, keepdims=True) + RMS_EPS) * g.astype(F32)).astype(x.dtype)


def rope_tables(pos):
    inv = 1.0 / (ROPE_THETA ** (jnp.arange(0, D_ROPE, 2, dtype=F32) / D_ROPE))
    ang = pos.astype(F32)[:, None] * inv[None, :]
    return jnp.cos(ang), jnp.sin(ang)


def apply_rope(x, cos, sin):
    xf = x.astype(F32)
    x1, x2 = xf[..., :D_ROPE // 2], xf[..., D_ROPE // 2:]
    return jnp.concatenate([x1 * cos - x2 * sin, x1 * sin + x2 * cos], -1).astype(x.dtype)


def mla_project(x, pos, w_in, g_q, g_kv, w_uq):
    b, l, _ = x.shape
    h = x @ w_in
    c_q = rms_norm(h[..., :Q_LORA], g_q)
    c_kv = rms_norm(h[..., Q_LORA:Q_LORA + KV_LORA], g_kv)
    cos, sin = rope_tables(pos)
    k_r = apply_rope(h[..., Q_LORA + KV_LORA:], cos, sin)
    q = (c_q @ w_uq).reshape(b, l, N_HEADS, D_NOPE + D_ROPE)
    q_nope = q[..., :D_NOPE]
    q_r = apply_rope(q[..., D_NOPE:], cos[:, None], sin[:, None])
    return q_nope, q_r, c_kv, k_r


def mla_prompt(x, w_in, g_q, g_kv, w_uq, w_uk, w_uv, w_o):
    b, l, _ = x.shape
    pos = jnp.arange(l)
    q_nope, q_r, c_kv, k_r = mla_project(x, pos, w_in, g_q, g_kv, w_uq)
    k_nope = (c_kv @ w_uk).reshape(b, l, N_HEADS, D_NOPE)
    v = (c_kv @ w_uv).reshape(b, l, N_HEADS, D_V)
    nb = l // Q_BLOCK
    qn_b = q_nope.reshape(b, nb, Q_BLOCK, N_HEADS, D_NOPE).swapaxes(0, 1)
    qr_b = q_r.reshape(b, nb, Q_BLOCK, N_HEADS, D_ROPE).swapaxes(0, 1)

    def block(args):
        i, qn, qr = args
        s = (jnp.einsum('bqhd,bkhd->bhqk', qn, k_nope)
             + jnp.einsum('bqhr,bkr->bhqk', qr, k_r)).astype(F32) * ATTN_SCALE
        q_pos = i * Q_BLOCK + jnp.arange(Q_BLOCK)
        s = jnp.where(pos[None, :] <= q_pos[:, None], s, -jnp.inf)
        p = jax.nn.softmax(s, axis=-1).astype(v.dtype)
        return jnp.einsum('bhqk,bkhv->bqhv', p, v)

    o = lax.map(block, (jnp.arange(nb), qn_b, qr_b))
    o = o.swapaxes(0, 1).reshape(b, l, N_HEADS * D_V)
    return o @ w_o, c_kv, k_r


def mla_sample(x, cache_lat, cache_kr, page_table, layer, w_in, g_q, g_kv, w_uq, w_uk, w_uv, w_o):
    bd, t, _ = x.shape
    pos = PAST_LEN + jnp.arange(t)
    q_nope, q_r, c_kv, k_r = mla_project(x, pos, w_in, g_q, g_kv, w_uq)
    q_lat = jnp.einsum('bthd,chd->bthc', q_nope, w_uk.reshape(KV_LORA, N_HEADS, D_NOPE))

    def page_step(carry, phys):
        m, l, acc = carry
        c_pg = cache_lat[layer, phys]
        r_pg = cache_kr[layer, phys]
        s = (jnp.einsum('bthc,bjc->bthj', q_lat, c_pg.astype(q_lat.dtype))
             + jnp.einsum('bthr,bjr->bthj', q_r, r_pg.astype(q_r.dtype))).astype(F32) * ATTN_SCALE
        m_new = jnp.maximum(m, jnp.max(s, -1))
        corr = jnp.exp(m - m_new)
        p = jnp.exp(s - m_new[..., None])
        l = l * corr + jnp.sum(p, -1)
        acc = acc * corr[..., None] + jnp.einsum('bthj,bjc->bthc', p, c_pg.astype(F32))
        return (m_new, l, acc), None

    init = (jnp.full((bd, t, N_HEADS), -1e30, F32),
            jnp.zeros((bd, t, N_HEADS), F32),
            jnp.zeros((bd, t, N_HEADS, KV_LORA), F32))
    (m, l, acc), _ = lax.scan(page_step, init, page_table.T)
    s_new = (jnp.einsum('bthc,bjc->bthj', q_lat, c_kv)
             + jnp.einsum('bthr,bjr->bthj', q_r, k_r)).astype(F32) * ATTN_SCALE
    causal = jnp.arange(t)[None, :] <= jnp.arange(t)[:, None]
    s_new = jnp.where(causal[None, :, None, :], s_new, -jnp.inf)
    m_new = jnp.maximum(m, jnp.max(s_new, -1))
    corr = jnp.exp(m - m_new)
    p = jnp.exp(s_new - m_new[..., None])
    l = l * corr + jnp.sum(p, -1)
    acc = acc * corr[..., None] + jnp.einsum('bthj,bjc->bthc', p, c_kv.astype(F32))
    o_lat = (acc / l[..., None]).astype(x.dtype)
    o = jnp.einsum('bthc,chv->bthv', o_lat, w_uv.reshape(KV_LORA, N_HEADS, D_V)).reshape(bd, t, N_HEADS * D_V)
    return o @ w_o, c_kv, k_r


def s5_discretize(a_re, a_im, log_dt, b_re, b_im):
    lr = jnp.minimum(a_re.astype(F32), -1e-4)
    li = a_im.astype(F32)
    dt = jnp.exp(log_dt.astype(F32))[:, None]
    mag = jnp.exp(lr * dt)
    ab_re = mag * jnp.cos(li * dt)
    ab_im = mag * jnp.sin(li * dt)
    den = lr * lr + li * li
    nr, ni = ab_re - 1.0, ab_im
    f_re = ((nr * lr + ni * li) / den)[..., None]
    f_im = ((ni * lr - nr * li) / den)[..., None]
    br, bi = b_re.astype(F32), b_im.astype(F32)
    bb_re = f_re * br - f_im * bi
    bb_im = f_re * bi + f_im * br
    return ab_re, ab_im, bb_re, bb_im


def complex_affine_combine(e1, e2):
    a1r, a1i, b1r, b1i = e1
    a2r, a2i, b2r, b2i = e2
    return (a2r * a1r - a2i * a1i,
            a2r * a1i + a2i * a1r,
            a2r * b1r - a2i * b1i + b2r,
            a2r * b1i + a2i * b1r + b2i)


def s5_mix(x, s0_re, s0_im, w_in, a_re, a_im, log_dt, b_re, b_im, c_re, c_im, d_skip, w_glu, w_o):
    bn, l, _ = x.shape
    u = (x @ w_in).astype(F32).reshape(bn, l, N_GROUPS, GROUP_SIZE)
    ab_re, ab_im, bb_re, bb_im = s5_discretize(a_re, a_im, log_dt, b_re, b_im)
    bu_re = jnp.einsum('blgp,gnp->blgn', u, bb_re)
    bu_im = jnp.einsum('blgp,gnp->blgn', u, bb_im)
    s0r, s0i = s0_re.astype(F32), s0_im.astype(F32)
    bu_re = bu_re.at[:, 0].add(ab_re * s0r - ab_im * s0i)
    bu_im = bu_im.at[:, 0].add(ab_re * s0i + ab_im * s0r)
    a_r = jnp.broadcast_to(ab_re, bu_re.shape)
    a_i = jnp.broadcast_to(ab_im, bu_re.shape)
    _, _, s_re, s_im = lax.associative_scan(complex_affine_combine, (a_r, a_i, bu_re, bu_im), axis=1)
    y = (jnp.einsum('blgn,gpn->blgp', s_re, c_re.astype(F32))
         - jnp.einsum('blgn,gpn->blgp', s_im, c_im.astype(F32))
         + d_skip.astype(F32) * u)
    g = jax.nn.gelu(y.reshape(bn, l, D_MODEL).astype(x.dtype))
    ga = g @ w_glu
    h = ga[..., :D_MODEL] * jax.nn.sigmoid(ga[..., D_MODEL:])
    return h @ w_o, s_re[:, -1], s_im[:, -1]


def peer_ffn(x, w_q, sub_keys, exp_u, exp_v):
    shp = x.shape
    xt = x.reshape(-1, D_MODEL)
    n = xt.shape[0]
    nb = -(-n // PEER_BLOCK)
    xp = jnp.pad(xt, ((0, nb * PEER_BLOCK - n), (0, 0))).reshape(nb, PEER_BLOCK, D_MODEL)

    def block(xb):
        q = (xb @ w_q).reshape(PEER_BLOCK, PEER_HEADS, 2, HALF_KEY)
        s = jnp.einsum('thsk,hsnk->thsn', q, sub_keys).astype(F32)
        v_half, i_half = lax.top_k(s, PEER_TOPK)
        cand = v_half[:, :, 0, :, None] + v_half[:, :, 1, None, :]
        v_top, i_top = lax.top_k(cand.reshape(PEER_BLOCK, PEER_HEADS, PEER_TOPK * PEER_TOPK), PEER_TOPK)
        i1 = jnp.take_along_axis(i_half[:, :, 0], i_top // PEER_TOPK, -1)
        i2 = jnp.take_along_axis(i_half[:, :, 1], i_top % PEER_TOPK, -1)
        expert = i1 * N_KEYS + i2
        gate = jax.nn.softmax(v_top, axis=-1)
        u = exp_u[expert]
        v = exp_v[expert]
        act = jax.nn.gelu(jnp.einsum('thkd,td->thk', u, xb).astype(F32))
        return jnp.einsum('thk,thkd->td', (gate * act).astype(xb.dtype), v)

    out = lax.map(block, xp).reshape(-1, D_MODEL)[:n]
    return out.reshape(shp)


def setup_inputs(seed: int = 0) -> dict:
    key = jax.random.key(seed)
    ks = iter(jax.random.split(key, 40))

    def nrm(shape, scale):
        return jax.random.normal(next(ks), shape, F32) * scale

    n_pages = PAST_LEN // PAGE_SIZE
    n_used = DEC_BATCH * n_pages
    n_pool = (5 * n_used) // 4
    perm = jax.random.permutation(next(ks), n_pool)
    page_table = perm[:n_used].reshape(DEC_BATCH, n_pages).astype(jnp.int32)

    cache_kv_latent = nrm((N_MLA_LAYERS, n_pool, PAGE_SIZE, KV_LORA), 1.0)
    cache_k_rope = nrm((N_MLA_LAYERS, n_pool, PAGE_SIZE, D_ROPE), 1.0)
    state_ssm_re = nrm((N_SSM_LAYERS, DEC_BATCH, N_GROUPS, STATE), 0.1)
    state_ssm_im = nrm((N_SSM_LAYERS, DEC_BATCH, N_GROUPS, STATE), 0.1)
    x_prompt = nrm((BATCH, SEQ, D_MODEL), 1.0)
    x_sample = nrm((DEC_BATCH, DEC_SEQ, D_MODEL), 1.0)

    d_in = D_MODEL ** -0.5
    mla_w_in = nrm((N_MLA_LAYERS, D_MODEL, Q_LORA + KV_LORA + D_ROPE), d_in)
    mla_g_q = 1.0 + nrm((N_MLA_LAYERS, Q_LORA), 0.01)
    mla_g_kv = 1.0 + nrm((N_MLA_LAYERS, KV_LORA), 0.01)
    mla_w_uq = nrm((N_MLA_LAYERS, Q_LORA, N_HEADS * (D_NOPE + D_ROPE)), Q_LORA ** -0.5)
    mla_w_uk = nrm((N_MLA_LAYERS, KV_LORA, N_HEADS * D_NOPE), KV_LORA ** -0.5)
    mla_w_uv = nrm((N_MLA_LAYERS, KV_LORA, N_HEADS * D_V), KV_LORA ** -0.5 * BETA)
    mla_w_o = nrm((N_MLA_LAYERS, N_HEADS * D_V, D_MODEL), (N_HEADS * D_V) ** -0.5 * BETA)

    ssm_w_in = nrm((N_SSM_LAYERS, D_MODEL, D_MODEL), d_in)
    ssm_a_re = -0.5 + nrm((N_SSM_LAYERS, N_GROUPS, STATE), 0.01)
    ssm_a_im = math.pi * jnp.arange(STATE, dtype=F32) + nrm((N_SSM_LAYERS, N_GROUPS, STATE), 0.01)
    ssm_log_dt = jax.random.uniform(next(ks), (N_SSM_LAYERS, N_GROUPS), F32,
                                    minval=math.log(DT_MIN), maxval=math.log(DT_MAX))
    ssm_b_re = nrm((N_SSM_LAYERS, N_GROUPS, STATE, GROUP_SIZE), (2 * GROUP_SIZE) ** -0.5)
    ssm_b_im = nrm((N_SSM_LAYERS, N_GROUPS, STATE, GROUP_SIZE), (2 * GROUP_SIZE) ** -0.5)
    ssm_c_re = nrm((N_SSM_LAYERS, N_GROUPS, GROUP_SIZE, STATE), STATE ** -0.5)
    ssm_c_im = nrm((N_SSM_LAYERS, N_GROUPS, GROUP_SIZE, STATE), STATE ** -0.5)
    ssm_d = nrm((N_SSM_LAYERS, N_GROUPS, GROUP_SIZE), 1.0)
    ssm_w_glu = nrm((N_SSM_LAYERS, D_MODEL, 2 * D_MODEL), d_in)
    ssm_w_o = nrm((N_SSM_LAYERS, D_MODEL, D_MODEL), d_in * BETA)

    peer_w_q = nrm((DEPTH, D_MODEL, PEER_HEADS * D_KEY), d_in)
    peer_sub_keys = nrm((DEPTH, PEER_HEADS, 2, N_KEYS, HALF_KEY), HALF_KEY ** -0.5)
    peer_u = nrm((DEPTH, N_EXPERTS, D_MODEL), d_in)
    peer_v = nrm((DEPTH, N_EXPERTS, D_MODEL), BETA)
    ln_g = 1.0 + nrm((DEPTH, 2, D_MODEL), 0.01)
    ln_b = nrm((DEPTH, 2, D_MODEL), 0.01)

    return {
        "x_prompt": x_prompt, "x_sample": x_sample,
        "cache_kv_latent": cache_kv_latent, "cache_k_rope": cache_k_rope,
        "state_ssm_re": state_ssm_re, "state_ssm_im": state_ssm_im,
        "page_table": page_table,
        "mla_w_in": mla_w_in, "mla_g_q": mla_g_q, "mla_g_kv": mla_g_kv, "mla_w_uq": mla_w_uq,
        "mla_w_uk": mla_w_uk, "mla_w_uv": mla_w_uv, "mla_w_o": mla_w_o,
        "ssm_w_in": ssm_w_in, "ssm_a_re": ssm_a_re, "ssm_a_im": ssm_a_im, "ssm_log_dt": ssm_log_dt,
        "ssm_b_re": ssm_b_re, "ssm_b_im": ssm_b_im, "ssm_c_re": ssm_c_re, "ssm_c_im": ssm_c_im,
        "ssm_d": ssm_d, "ssm_w_glu": ssm_w_glu, "ssm_w_o": ssm_w_o,
        "peer_w_q": peer_w_q, "peer_sub_keys": peer_sub_keys, "peer_u": peer_u, "peer_v": peer_v,
        "ln_g": ln_g, "ln_b": ln_b,
    }


def reference(x_prompt, x_sample, cache_kv_latent, cache_k_rope, state_ssm_re, state_ssm_im, page_table,
              mla_w_in, mla_g_q, mla_g_kv, mla_w_uq, mla_w_uk, mla_w_uv, mla_w_o,
              ssm_w_in, ssm_a_re, ssm_a_im, ssm_log_dt, ssm_b_re, ssm_b_im, ssm_c_re, ssm_c_im,
              ssm_d, ssm_w_glu, ssm_w_o,
              peer_w_q, peer_sub_keys, peer_u, peer_v, ln_g, ln_b):
    yp, ys = x_prompt, x_sample
    p_lat, p_kr, p_sre, p_sim = [], [], [], []
    s_lat, s_kr, s_sre, s_sim = [], [], [], []
    for layer in range(DEPTH):
        j = layer // N_MIXERS
        if layer % N_MIXERS == 0:
            w = (mla_w_in[j], mla_g_q[j], mla_g_kv[j], mla_w_uq[j], mla_w_uk[j], mla_w_uv[j], mla_w_o[j])
            mp, c_p, r_p = mla_prompt(yp, *w)
            ms, c_s, r_s = mla_sample(ys, cache_kv_latent, cache_k_rope, page_table, j, *w)
            p_lat.append(c_p); p_kr.append(r_p); s_lat.append(c_s); s_kr.append(r_s)
        else:
            w = (ssm_w_in[j], ssm_a_re[j], ssm_a_im[j], ssm_log_dt[j], ssm_b_re[j], ssm_b_im[j],
                 ssm_c_re[j], ssm_c_im[j], ssm_d[j], ssm_w_glu[j], ssm_w_o[j])
            zeros = jnp.zeros((yp.shape[0], N_GROUPS, STATE), F32)
            mp, sr_p, si_p = s5_mix(yp, zeros, zeros, *w)
            ms, sr_s, si_s = s5_mix(ys, state_ssm_re[j], state_ssm_im[j], *w)
            p_sre.append(sr_p); p_sim.append(si_p); s_sre.append(sr_s); s_sim.append(si_s)
        yp = layer_norm(ALPHA * yp + mp, ln_g[layer, 0], ln_b[layer, 0])
        ys = layer_norm(ALPHA * ys + ms, ln_g[layer, 0], ln_b[layer, 0])
        pw = (peer_w_q[layer], peer_sub_keys[layer], peer_u[layer], peer_v[layer])
        yp = layer_norm(ALPHA * yp + peer_ffn(yp, *pw), ln_g[layer, 1], ln_b[layer, 1])
        ys = layer_norm(ALPHA * ys + peer_ffn(ys, *pw), ln_g[layer, 1], ln_b[layer, 1])
    return (yp, ys,
            jnp.stack(p_lat), jnp.stack(p_kr), jnp.stack(p_sre), jnp.stack(p_sim),
            jnp.stack(s_lat), jnp.stack(s_kr), jnp.stack(s_sre), jnp.stack(s_sim))
```

```python
import functools
import math

import jax
import jax.numpy as jnp
from jax import lax
from jax.experimental import pallas as pl
from jax.experimental.pallas import tpu as pltpu

F32 = jnp.float32
BF16 = jnp.bfloat16

D_MODEL = 1024
DEPTH = 2
PAST_LEN = 16384
PAGE_SIZE = 128

N_HEADS = 8
D_NOPE = 128
D_ROPE = 64
D_V = 128
Q_LORA = 384
KV_LORA = 256
ROPE_THETA = 10000.0
ATTN_SCALE = 1.0 / math.sqrt(D_NOPE + D_ROPE)

GROUP_SIZE = 16
N_GROUPS = D_MODEL // GROUP_SIZE
STATE = 64
N_STATE = N_GROUPS * STATE

PEER_HEADS = 8
N_KEYS = 128
N_EXPERTS = N_KEYS * N_KEYS
HALF_KEY = 128
PEER_TOPK = 16

ALPHA = (2 * DEPTH) ** 0.25
LN_EPS = 1e-5
RMS_EPS = 1e-6

LANES = 128
SUBLANES = 8
VMEM_LIMIT = 56 * 1024 * 1024

NEG_BIG = -1e30


def _cparams(sem):
    return pltpu.CompilerParams(dimension_semantics=sem, vmem_limit_bytes=VMEM_LIMIT)


def _dot(a, b):
    return jnp.dot(a, b, preferred_element_type=F32)


def _dot_nt(a, b):
    return lax.dot_general(a, b, (((1,), (1,)), ((), ())), preferred_element_type=F32)


def _layer_norm(z, g, b):
    mu = jnp.mean(z, -1, keepdims=True)
    zc = z - mu
    var = jnp.mean(zc * zc, -1, keepdims=True)
    return zc * lax.rsqrt(var + LN_EPS) * g + b


def _rms_norm(x, g):
    return x * lax.rsqrt(jnp.mean(x * x, -1, keepdims=True) + RMS_EPS) * g


def _gelu_tanh(x):
    c = math.sqrt(2.0 / math.pi)
    return x * (0.5 * (1.0 + jnp.tanh(c * (x + 0.044715 * (x * x * x)))))


def _const_spec(shape):
    n = len(shape)
    return pl.BlockSpec(shape, lambda *_: (0,) * n)


def _mla_proj_kernel(absorb, x_ref, wcq_ref, wckv_ref, wkr_ref, gq_ref, gkv_ref,
                     wuqn_ref, wuqr_ref, wuqs_ref, cs_ref, wa_ref, wb_ref, *out_refs):
    xb = x_ref[...].astype(BF16)
    c_q = _rms_norm(_dot(xb, wcq_ref[...]), gq_ref[...])
    c_kv = _rms_norm(_dot(xb, wckv_ref[...]), gkv_ref[...])
    hk = _dot(xb, wkr_ref[...])
    cos = cs_ref[:, :LANES]
    sin = cs_ref[:, LANES:]
    kr = hk[:, :LANES] * cos + hk[:, LANES:] * sin
    cqb = c_q.astype(BF16)
    cos8 = jnp.concatenate([cos] * N_HEADS, axis=1)
    sin8 = jnp.concatenate([sin] * N_HEADS, axis=1)
    qn = _dot(cqb, wuqn_ref[...]) * ATTN_SCALE
    qr = (_dot(cqb, wuqr_ref[...]) * cos8 + _dot(cqb, wuqs_ref[...]) * sin8) * ATTN_SCALE
    ckvb = c_kv.astype(BF16)
    if absorb:
        ckv_ref, kr_ref, qlat_ref, qr_ref = out_refs
        qnb = qn.astype(BF16)
        for h in range(N_HEADS):
            qlat_ref[:, h * KV_LORA:(h + 1) * KV_LORA] = _dot(
                qnb[:, h * D_NOPE:(h + 1) * D_NOPE], wa_ref[h * D_NOPE:(h + 1) * D_NOPE, :]).astype(BF16)
    else:
        ckv_ref, kr_ref, qn_ref, qr_ref, kn_ref, krp_ref, v_ref = out_refs
        qn_ref[...] = qn.astype(BF16)
        kn_ref[...] = _dot(ckvb, wa_ref[...]).astype(BF16)
        v_ref[...] = _dot(ckvb, wb_ref[...]).astype(BF16)
        krp_ref[...] = kr.astype(BF16)
    ckv_ref[...] = c_kv
    kr_ref[...] = kr[:, :D_ROPE]
    qr_ref[...] = qr.astype(BF16)


def _rope_cs(pos):
    inv = 1.0 / (ROPE_THETA ** (jnp.arange(0, D_ROPE, 2, dtype=F32) / D_ROPE))
    ang = pos.astype(F32)[:, None] * inv[None, :]
    cos, sin = jnp.cos(ang), jnp.sin(ang)
    z = jnp.zeros((pos.shape[0], LANES - D_ROPE), F32)
    return jnp.concatenate([cos, cos, z, -sin, sin, z], axis=1)


def _mla_weights(w_in, g_q, g_kv, w_uq, w_uk, w_uv, w_o):
    half = D_ROPE // 2
    w_cq = w_in[:, :Q_LORA].astype(BF16)
    w_ckv = w_in[:, Q_LORA:Q_LORA + KV_LORA].astype(BF16)
    w_k = w_in[:, Q_LORA + KV_LORA:]
    zk = jnp.zeros((D_MODEL, LANES - D_ROPE), F32)
    w_kr = jnp.concatenate([w_k, zk, w_k[:, half:], w_k[:, :half], zk], axis=1).astype(BF16)
    wq = w_uq.reshape(Q_LORA, N_HEADS, D_NOPE + D_ROPE)
    wq_n = wq[:, :, :D_NOPE].reshape(Q_LORA, N_HEADS * D_NOPE).astype(BF16)
    r = wq[:, :, D_NOPE:]
    zq = jnp.zeros((Q_LORA, N_HEADS, LANES - D_ROPE), F32)
    wq_r = jnp.concatenate([r, zq], axis=2).reshape(Q_LORA, N_HEADS * LANES).astype(BF16)
    wq_s = jnp.concatenate([r[:, :, half:], r[:, :, :half], zq], axis=2).reshape(Q_LORA, N_HEADS * LANES).astype(BF16)
    return dict(w_cq=w_cq, w_ckv=w_ckv, w_kr=w_kr, g_q=g_q.reshape(1, -1), g_kv=g_kv.reshape(1, -1),
                wq_n=wq_n, wq_r=wq_r, wq_s=wq_s, w_uk=w_uk.astype(BF16), w_ukT=w_uk.T.astype(BF16),
                w_uv=w_uv.astype(BF16), w_o=w_o.astype(BF16))


def _mla_proj(x, cs, w, absorb, tn):
    n = x.shape[0]
    hw = N_HEADS * LANES
    row = lambda width: pl.BlockSpec((tn, width), lambda i: (i, 0))
    wa = w["w_ukT"] if absorb else w["w_uk"]
    wb = w["w_uv"]
    in_specs = [row(D_MODEL), _const_spec(w["w_cq"].shape), _const_spec(w["w_ckv"].shape),
                _const_spec(w["w_kr"].shape), _const_spec(w["g_q"].shape), _const_spec(w["g_kv"].shape),
                _const_spec(w["wq_n"].shape), _const_spec(w["wq_r"].shape), _const_spec(w["wq_s"].shape),
                row(2 * LANES), _const_spec(wa.shape), _const_spec(wb.shape)]
    if absorb:
        out_shape = [jax.ShapeDtypeStruct((n, KV_LORA), F32), jax.ShapeDtypeStruct((n, D_ROPE), F32),
                     jax.ShapeDtypeStruct((n, N_HEADS * KV_LORA), BF16), jax.ShapeDtypeStruct((n, hw), BF16)]
        out_specs = [row(KV_LORA), row(D_ROPE), row(N_HEADS * KV_LORA), row(hw)]
    else:
        out_shape = [jax.ShapeDtypeStruct((n, KV_LORA), F32), jax.ShapeDtypeStruct((n, D_ROPE), F32),
                     jax.ShapeDtypeStruct((n, hw), BF16), jax.ShapeDtypeStruct((n, hw), BF16),
                     jax.ShapeDtypeStruct((n, hw), BF16), jax.ShapeDtypeStruct((n, LANES), BF16),
                     jax.ShapeDtypeStruct((n, hw), BF16)]
        out_specs = [row(KV_LORA), row(D_ROPE), row(hw), row(hw), row(hw), row(LANES), row(hw)]
    return pl.pallas_call(
        functools.partial(_mla_proj_kernel, absorb),
        grid=(n // tn,), in_specs=in_specs, out_specs=out_specs, out_shape=out_shape,
        compiler_params=_cparams(("arbitrary",)), name="mla_proj_absorb" if absorb else "mla_proj",
    )(x, w["w_cq"], w["w_ckv"], w["w_kr"], w["g_q"], w["g_kv"], w["wq_n"], w["wq_r"], w["wq_s"], cs, wa, wb)


def _flash_kernel(qn_ref, qr_ref, kn_ref, kr_ref, v_ref, x_ref, wo_ref, g_ref, b_ref, o_ref,
                  m_ref, l_ref, acc_ref):
    qi = pl.program_id(0)
    ki = pl.program_id(1)
    tq = qn_ref.shape[0]
    tk = kn_ref.shape[0]

    @pl.when(ki == 0)
    def _():
        m_ref[...] = jnp.full(m_ref.shape, NEG_BIG, F32)
        l_ref[...] = jnp.zeros(l_ref.shape, F32)
        acc_ref[...] = jnp.zeros(acc_ref.shape, F32)

    def step(masked):
        kr = kr_ref[...]
        if masked:
            keep = (lax.broadcasted_iota(jnp.int32, (tq, tk), 1)
                    <= lax.broadcasted_iota(jnp.int32, (tq, tk), 0))
        for h in range(N_HEADS):
            hs = slice(h * LANES, (h + 1) * LANES)
            q = jnp.concatenate([qn_ref[:, hs], qr_ref[:, hs]], axis=1)
            k = jnp.concatenate([kn_ref[:, hs], kr], axis=1)
            s = _dot_nt(q, k)
            if masked:
                s = jnp.where(keep, s, -jnp.inf)
            m_prev = m_ref[h]
            m_new = jnp.maximum(m_prev, jnp.max(s, -1, keepdims=True))
            p = jnp.exp(s - m_new)
            corr = jnp.exp(m_prev - m_new)
            l_ref[h] = corr * l_ref[h] + jnp.sum(p, -1, keepdims=True)
            acc_ref[h] = corr * acc_ref[h] + _dot(p.astype(BF16), v_ref[:, hs])
            m_ref[h] = m_new

    @pl.when(ki < qi)
    def _():
        step(False)

    @pl.when(ki == qi)
    def _():
        step(True)
        o = jnp.concatenate([acc_ref[h] / l_ref[h] for h in range(N_HEADS)], axis=1)
        z = ALPHA * x_ref[...] + _dot(o.astype(BF16), wo_ref[...])
        o_ref[...] = _layer_norm(z, g_ref[...], b_ref[...])


def _flash_prompt(qn, qr, kn, krp, v, x, w_o, g, b, tq):
    n = x.shape[0]
    hw = N_HEADS * LANES
    qspec = lambda width: pl.BlockSpec((tq, width), lambda i, j: (i, 0))
    kspec = lambda width: pl.BlockSpec((tq, width), lambda i, j: (jnp.minimum(i, j), 0))
    return pl.pallas_call(
        _flash_kernel,
        grid=(n // tq, n // tq),
        in_specs=[qspec(hw), qspec(hw), kspec(hw), kspec(LANES), kspec(hw), qspec(D_MODEL),
                  _const_spec(w_o.shape), _const_spec(g.shape), _const_spec(b.shape)],
        out_specs=qspec(D_MODEL),
        out_shape=jax.ShapeDtypeStruct((n, D_MODEL), F32),
        scratch_shapes=[pltpu.VMEM((N_HEADS, tq, 1), F32), pltpu.VMEM((N_HEADS, tq, 1), F32),
                        pltpu.VMEM((N_HEADS, tq, D_V), F32)],
        compiler_params=_cparams(("arbitrary", "arbitrary")), name="mla_flash_prompt",
    )(qn, qr, kn, krp, v, x, w_o, g, b)


PAGES_PER_STEP = 16
ROWS = 32


def _decode_kernel(pt_ref, ql_ref, qr_ref, cn_ref, rn_ref, *rest):
    lat_refs = rest[:PAGES_PER_STEP]
    rope_refs = rest[PAGES_PER_STEP:2 * PAGES_PER_STEP]
    o_ref, m_ref, l_ref, acc_ref = rest[2 * PAGES_PER_STEP:]
    c = pl.program_id(1)
    nc = pl.num_programs(1)

    @pl.when(c == 0)
    def _():
        m_ref[...] = jnp.full(m_ref.shape, NEG_BIG, F32)
        l_ref[...] = jnp.zeros(l_ref.shape, F32)
        acc_ref[...] = jnp.zeros(acc_ref.shape, F32)

    ql = ql_ref[...]
    qr = qr_ref[:, :D_ROPE]
    lat = [r[...].astype(BF16) for r in lat_refs]
    s = [_dot_nt(ql, lat[p]) + _dot_nt(qr, rope_refs[p][...].astype(BF16)) for p in range(PAGES_PER_STEP)]
    m_prev = m_ref[...]
    m_new = m_prev
    for sp in s:
        m_new = jnp.maximum(m_new, jnp.max(sp, -1, keepdims=True))
    corr = jnp.exp(m_prev - m_new)
    l_new = corr * l_ref[...]
    acc = corr * acc_ref[...]
    for p in range(PAGES_PER_STEP):
        pp = jnp.exp(s[p] - m_new)
        l_new = l_new + jnp.sum(pp, -1, keepdims=True)
        acc = acc + _dot(pp.astype(BF16), lat[p])
    m_ref[...] = m_new
    l_ref[...] = l_new
    acc_ref[...] = acc

    @pl.when(c == nc - 1)
    def _():
        qlf = ql.astype(F32)
        qrf = qr.astype(F32)
        cn = cn_ref[...].astype(BF16).astype(F32)
        rn = rn_ref[...].astype(BF16).astype(F32)
        t_of_row = lax.broadcasted_iota(jnp.int32, (ROWS, 1), 0) // N_HEADS
        n_new = cn.shape[0]
        s_new = []
        for j in range(n_new):
            sj = (jnp.sum(qlf * cn[j:j + 1, :], -1, keepdims=True)
                  + jnp.sum(qrf * rn[j:j + 1, :], -1, keepdims=True))
            s_new.append(jnp.where(j <= t_of_row, sj, -jnp.inf))
        m0 = m_ref[...]
        m1 = m0
        for sj in s_new:
            m1 = jnp.maximum(m1, sj)
        corr1 = jnp.exp(m0 - m1)
        l1 = corr1 * l_ref[...]
        a1 = corr1 * acc_ref[...]
        for j in range(n_new):
            pj = jnp.exp(s_new[j] - m1)
            l1 = l1 + pj
            a1 = a1 + pj.astype(BF16).astype(F32) * cn[j:j + 1, :]
        o_ref[...] = a1 / l1


def _decode(qlat, qr, ckv_new, kr_new, cache_lat, cache_kr, page_table):
    nb, n_pages = page_table.shape
    nc = n_pages // PAGES_PER_STEP
    n_new = ckv_new.shape[1]

    def page_spec(p, width):
        return pl.BlockSpec((None, None, PAGE_SIZE, width),
                            lambda b, c, pt: (0, pt[b, c * PAGES_PER_STEP + p], 0, 0))

    per_b = lambda rows, width: pl.BlockSpec((None, rows, width), lambda b, c, pt: (b, 0, 0))
    in_specs = ([per_b(ROWS, KV_LORA), per_b(ROWS, LANES), per_b(n_new, KV_LORA), per_b(n_new, D_ROPE)]
                + [page_spec(p, KV_LORA) for p in range(PAGES_PER_STEP)]
                + [page_spec(p, D_ROPE) for p in range(PAGES_PER_STEP)])
    grid_spec = pltpu.PrefetchScalarGridSpec(
        num_scalar_prefetch=1, grid=(nb, nc), in_specs=in_specs,
        out_specs=per_b(ROWS, KV_LORA),
        scratch_shapes=[pltpu.VMEM((ROWS, 1), F32), pltpu.VMEM((ROWS, 1), F32), pltpu.VMEM((ROWS, KV_LORA), F32)])
    return pl.pallas_call(
        _decode_kernel, grid_spec=grid_spec,
        out_shape=jax.ShapeDtypeStruct((nb, ROWS, KV_LORA), F32),
        compiler_params=_cparams(("arbitrary", "arbitrary")), name="mla_decode",
    )(page_table, qlat, qr, ckv_new, kr_new, *([cache_lat] * PAGES_PER_STEP), *([cache_kr] * PAGES_PER_STEP))


def _sample_out_kernel(ol_ref, x_ref, wuv_ref, wo_ref, g_ref, b_ref, o_ref):
    olb = ol_ref[...].astype(BF16)
    o = jnp.concatenate(
        [_dot(olb[:, h * KV_LORA:(h + 1) * KV_LORA], wuv_ref[:, h * D_V:(h + 1) * D_V]) for h in range(N_HEADS)],
        axis=1)
    z = ALPHA * x_ref[...] + _dot(o.astype(BF16), wo_ref[...])
    o_ref[...] = _layer_norm(z, g_ref[...], b_ref[...])


def _sample_out(o_lat, x, w_uv, w_o, g, b):
    n = x.shape[0]
    args = (o_lat, x, w_uv, w_o, g, b)
    return pl.pallas_call(
        _sample_out_kernel, grid=(1,), in_specs=[_const_spec(a.shape) for a in args],
        out_specs=_const_spec((n, D_MODEL)), out_shape=jax.ShapeDtypeStruct((n, D_MODEL), F32),
        compiler_params=_cparams(("arbitrary",)), name="mla_sample_out",
    )(*args)


def _sort16_network():
    def merge(lo, hi, r):
        step = r * 2
        if step < hi - lo:
            yield from merge(lo, hi, step)
            yield from merge(lo + r, hi, step)
            yield from [(i, i + r) for i in range(lo + r, hi - r, step)]
        else:
            yield (lo, lo + r)

    def sort(lo, hi):
        if hi - lo >= 1:
            mid = lo + (hi - lo) // 2
            yield from sort(lo, mid)
            yield from sort(mid + 1, hi)
            yield from merge(lo, hi, 1)

    return tuple(sort(0, PEER_TOPK - 1))


_SORT16 = _sort16_network()


def _sort16_desc(v):
    v = list(v)
    for i, j in _SORT16:
        v[i], v[j] = jnp.maximum(v[i], v[j]), jnp.minimum(v[i], v[j])
    return v


def _bitonic_merge16_desc(v):
    v = list(v)
    d = PEER_TOPK // 2
    while d >= 1:
        for i in range(PEER_TOPK):
            if i & d == 0:
                v[i], v[i + d] = jnp.maximum(v[i], v[i + d]), jnp.minimum(v[i], v[i + d])
        d //= 2
    return v


def _merge_top16(x, y):
    return _bitonic_merge16_desc([jnp.maximum(x[k], y[PEER_TOPK - 1 - k]) for k in range(PEER_TOPK)])


def _top16_rows(s):
    v = _sort16_desc([s[SUBLANES * k:SUBLANES * (k + 1), :] for k in range(N_KEYS // SUBLANES)])
    for shift in (4, 2, 1):
        v = _merge_top16(v, [pltpu.roll(a, shift, 0) for a in v])
    return v


def _pair_region():
    return [(a, b) for a in range(PEER_TOPK) for b in range(PEER_TOPK) if (a + 1) * (b + 1) <= PEER_TOPK]


def _peer_route_kernel(x_ref, wqt_ref, keys_ref, s1_ref, e1_ref, s2_ref, e2_ref, tau_ref):
    tn = x_ref.shape[0]
    xb = x_ref[...].astype(BF16)
    qt = _dot_nt(wqt_ref[...], xb)
    sub = lax.broadcasted_iota(jnp.int32, (SUBLANES, tn), 0)
    tops = []
    packed = [[jnp.zeros((SUBLANES, tn), F32) for _ in range(PEER_TOPK)] for _ in range(2)]
    scores = []
    for h in range(PEER_HEADS):
        for half in range(2):
            c = 2 * h + half
            st = _dot(keys_ref[c], qt[c * HALF_KEY:(c + 1) * HALF_KEY, :].astype(BF16))
            scores.append(st)
            top = _top16_rows(st)
            tops.append(top[0])
            for k in range(PEER_TOPK):
                packed[half][k] = jnp.where(sub == h, top[k], packed[half][k])
    cands = [packed[0][a] + packed[1][b] for a, b in _pair_region()]
    pad = (-len(cands)) % PEER_TOPK
    cands += [jnp.full((SUBLANES, tn), -jnp.inf, F32)] * pad
    best = _sort16_desc(cands[:PEER_TOPK])
    for g in range(1, len(cands) // PEER_TOPK):
        best = _merge_top16(best, _sort16_desc(cands[g * PEER_TOPK:(g + 1) * PEER_TOPK]))
    tau = best[PEER_TOPK - 1]
    z = jnp.zeros((SUBLANES, tn), F32)
    for k in range(PEER_TOPK):
        z = z + jnp.exp(best[k] - best[0])
    inv_z = 1.0 / z
    tau_ref[...] = tau
    for h in range(PEER_HEADS):
        m1 = jnp.concatenate([tops[2 * h]] * (N_KEYS // SUBLANES), axis=0)
        m2 = jnp.concatenate([tops[2 * h + 1]] * (N_KEYS // SUBLANES), axis=0)
        s1_ref[h] = scores[2 * h]
        s2_ref[h] = scores[2 * h + 1]
        e1_ref[h] = jnp.exp(scores[2 * h] - m1) * inv_z[h:h + 1, :]
        e2_ref[h] = jnp.exp(scores[2 * h + 1] - m2)


def _peer_route(x, w_qt, keys, tn):
    n = x.shape[0]
    big = jax.ShapeDtypeStruct((PEER_HEADS, N_KEYS, n), F32)
    bspec = pl.BlockSpec((PEER_HEADS, N_KEYS, tn), lambda i: (0, 0, i))
    return pl.pallas_call(
        _peer_route_kernel, grid=(n // tn,),
        in_specs=[pl.BlockSpec((tn, D_MODEL), lambda i: (i, 0)), _const_spec(w_qt.shape), _const_spec(keys.shape)],
        out_specs=[bspec, bspec, bspec, bspec, pl.BlockSpec((PEER_HEADS, tn), lambda i: (0, i))],
        out_shape=[big, big, big, big, jax.ShapeDtypeStruct((PEER_HEADS, n), F32)],
        compiler_params=_cparams(("arbitrary",)), name="peer_route",
    )(x, w_qt, keys)


PEER_TE = SUBLANES * N_KEYS
PEER_LW = 2 * LANES


def _peer_dense_kernel(x_ref, u_ref, vt_ref, s1_ref, e1_ref, s2_ref, e2_ref, tau_ref, g_ref, b_ref,
                       o_ref, acc_ref, h_ref, w_ref):
    j = pl.program_id(1)
    tm = x_ref.shape[0]

    @pl.when(j == 0)
    def _():
        acc_ref[...] = jnp.zeros(acc_ref.shape, F32)

    h_ref[...] = _dot_nt(u_ref[...], x_ref[...].astype(BF16))

    for lc in range(tm // PEER_LW):
        ls = slice(lc * PEER_LW, (lc + 1) * PEER_LW)

        def jb_body(jb, carry, ls=ls):
            r = pl.multiple_of(jb * SUBLANES, SUBLANES)
            gates = [jnp.zeros((SUBLANES, PEER_LW), F32) for _ in range(SUBLANES)]
            for h in range(PEER_HEADS):
                s2 = s2_ref[h, pl.ds(r, SUBLANES), ls]
                e2 = e2_ref[h, pl.ds(r, SUBLANES), ls]
                tau = tau_ref[h:h + 1, ls]
                for ii in range(SUBLANES):
                    s1 = s1_ref[h, ii:ii + 1, ls]
                    e1 = e1_ref[h, ii:ii + 1, ls]
                    gates[ii] = gates[ii] + jnp.where(s2 + s1 >= tau, e2 * e1, 0.0)
            for ii in range(SUBLANES):
                rows = pl.ds(pl.multiple_of(ii * N_KEYS + r, SUBLANES), SUBLANES)
                w_ref[rows, ls] = _gelu_tanh(h_ref[rows, ls]) * gates[ii]
            return carry

        lax.fori_loop(0, N_KEYS // SUBLANES, jb_body, 0)

    acc_ref[...] += _dot(vt_ref[...], w_ref[...].astype(BF16))

    @pl.when(j == pl.num_programs(1) - 1)
    def _():
        z = ALPHA * x_ref[...] + acc_ref[...].T
        o_ref[...] = _layer_norm(z, g_ref[...], b_ref[...])


def _peer_dense(x, u, vt, s1, e1, s2, e2, tau, g, b, tm):
    n = x.shape[0]
    full = pl.BlockSpec((PEER_HEADS, N_KEYS, tm), lambda i, j: (0, 0, i))
    part = pl.BlockSpec((PEER_HEADS, SUBLANES, tm), lambda i, j: (0, j, i))
    return pl.pallas_call(
        _peer_dense_kernel, grid=(n // tm, N_EXPERTS // PEER_TE),
        in_specs=[pl.BlockSpec((tm, D_MODEL), lambda i, j: (i, 0)),
                  pl.BlockSpec((PEER_TE, D_MODEL), lambda i, j: (j, 0)),
                  pl.BlockSpec((D_MODEL, PEER_TE), lambda i, j: (0, j)),
                  part, part, full, full,
                  pl.BlockSpec((PEER_HEADS, tm), lambda i, j: (0, i)),
                  _const_spec(g.shape), _const_spec(b.shape)],
        out_specs=pl.BlockSpec((tm, D_MODEL), lambda i, j: (i, 0)),
        out_shape=jax.ShapeDtypeStruct((n, D_MODEL), F32),
        scratch_shapes=[pltpu.VMEM((D_MODEL, tm), F32), pltpu.VMEM((PEER_TE, tm), F32),
                        pltpu.VMEM((PEER_TE, tm), F32)],
        compiler_params=_cparams(("arbitrary", "arbitrary")), name="peer_dense",
    )(x, u, vt, s1, e1, s2, e2, tau, g, b)


def _peer_weights(w_q, sub_keys, exp_u, exp_v):
    return dict(w_qt=w_q.T.astype(BF16),
                keys=sub_keys.reshape(2 * PEER_HEADS, N_KEYS, HALF_KEY).astype(BF16),
                u=exp_u.astype(BF16), vt=exp_v.T.astype(BF16))


def _peer_ln(x, w, g, b, tn_route, tm):
    s1, e1, s2, e2, tau = _peer_route(x, w["w_qt"], w["keys"], tn_route)
    return _peer_dense(x, w["u"], w["vt"], s1, e1, s2, e2, tau, g, b, tm)


def _s5_disc_kernel(are_ref, aim_ref, ldt_ref, bre_ref, bim_ref, abre_ref, abim_ref, bbre_ref, bbim_ref):
    lr = jnp.minimum(are_ref[...], -1e-4)
    li = aim_ref[...]
    dt = jnp.exp(ldt_ref[...])
    mag = jnp.exp(lr * dt)
    ab_re = mag * jnp.cos(li * dt)
    ab_im = mag * jnp.sin(li * dt)
    den = lr * lr + li * li
    nr, ni = ab_re - 1.0, ab_im
    f_re = (nr * lr + ni * li) / den
    f_im = (ni * lr - nr * li) / den
    br, bi = bre_ref[...], bim_ref[...]
    abre_ref[...] = ab_re
    abim_ref[...] = ab_im
    bbre_ref[...] = f_re * br - f_im * bi
    bbim_ref[...] = f_re * bi + f_im * br


def _s5_weights(w_in, a_re, a_im, log_dt, b_re, b_im, c_re, c_im, d_skip, w_glu, w_o):
    rep = lambda a: jnp.repeat(a, GROUP_SIZE, axis=0)
    bt = lambda b: b.transpose(0, 2, 1).reshape(D_MODEL, STATE)
    args = (rep(a_re), rep(a_im), rep(log_dt.reshape(N_GROUPS, 1)), bt(b_re), bt(b_im))
    sd = jax.ShapeDtypeStruct((D_MODEL, STATE), F32)
    ab_re, ab_im, bb_re, bb_im = pl.pallas_call(
        _s5_disc_kernel, grid=(1,), in_specs=[_const_spec(a.shape) for a in args],
        out_specs=[_const_spec(sd.shape)] * 4, out_shape=[sd] * 4,
        compiler_params=_cparams(("arbitrary",)), name="s5_discretize")(*args)
    nblk = 4
    gl = N_GROUPS // nblk
    eye = jnp.eye(gl, dtype=F32)

    def bdiag_in(bb):
        t = bb.reshape(nblk, gl, GROUP_SIZE, STATE)
        return jnp.einsum('kgpn,gh->kgphn', t, eye).reshape(nblk, gl * GROUP_SIZE, gl * STATE).astype(BF16)

    def bdiag_out(c):
        t = c.reshape(nblk, gl, GROUP_SIZE, STATE)
        return jnp.einsum('kgpn,gh->kgnhp', t, eye).reshape(nblk, gl * STATE, gl * GROUP_SIZE)

    c_cat = jnp.concatenate([bdiag_out(c_re), -bdiag_out(c_im)], axis=1).astype(BF16)
    return dict(w_in=w_in.astype(BF16), b_re=bdiag_in(bb_re), b_im=bdiag_in(bb_im),
                a_re=ab_re[::GROUP_SIZE].reshape(1, N_STATE), a_im=ab_im[::GROUP_SIZE].reshape(1, N_STATE),
                c_cat=c_cat, d=d_skip.reshape(1, D_MODEL), w_glu=w_glu.astype(BF16), w_o=w_o.astype(BF16))


S5_LW = 2 * LANES
S5_NBLK = 4


def _cmul(ar, ai, br, bi):
    return ar * br - ai * bi, ar * bi + ai * br


def _s5_scan_kernel(seg, x_ref, win_ref, bre_ref, bim_ref, are_ref, aim_ref, ccat_ref, d_ref, *rest):
    if seg:
        s0re_ref, s0im_ref, g_ref, ore_ref, oim_ref, sre_ref, sim_ref, tab_ref = rest
    else:
        g_ref, ore_ref, oim_ref, sre_ref, sim_ref, tab_ref, car_ref = rest
    i = pl.program_id(0)
    tl = x_ref.shape[0]
    period = seg if seg else SUBLANES

    @pl.when(i == 0)
    def _():
        ar = jnp.broadcast_to(are_ref[...], (SUBLANES, N_STATE))
        ai = jnp.broadcast_to(aim_ref[...], (SUBLANES, N_STATE))
        rr = lax.broadcasted_iota(jnp.int32, (SUBLANES, N_STATE), 0) % period
        pr, pi = ar, ai
        powers = [(pr, pi)]
        for _ in range(SUBLANES - 1):
            pr, pi = _cmul(pr, pi, ar, ai)
            powers.append((pr, pi))
        for k, d in enumerate((1, 2, 4)):
            tab_ref[2 * k] = jnp.where(rr >= d, powers[d - 1][0], 0.0)
            tab_ref[2 * k + 1] = jnp.where(rr >= d, powers[d - 1][1], 0.0)
        cr = jnp.zeros((SUBLANES, N_STATE), F32)
        ci = jnp.zeros((SUBLANES, N_STATE), F32)
        for p in range(SUBLANES):
            cr = jnp.where(rr == p, powers[p][0], cr)
            ci = jnp.where(rr == p, powers[p][1], ci)
        tab_ref[6] = cr
        tab_ref[7] = ci
        if not seg:
            car_ref[...] = jnp.zeros(car_ref.shape, F32)

    u = _dot(x_ref[...].astype(BF16), win_ref[...])
    ub = u.astype(BF16)
    kin = D_MODEL // S5_NBLK
    kst = N_STATE // S5_NBLK
    for k in range(S5_NBLK):
        sre_ref[:, k * kst:(k + 1) * kst] = _dot(ub[:, k * kin:(k + 1) * kin], bre_ref[k])
        sim_ref[:, k * kst:(k + 1) * kst] = _dot(ub[:, k * kin:(k + 1) * kin], bim_ref[k])

    steps = tuple(d for d in (1, 2, 4) if d < period)
    for lc in range(N_STATE // S5_LW):
        ls = slice(lc * S5_LW, (lc + 1) * S5_LW)
        tabs = [tab_ref[t, :, ls] for t in range(8)]

        def body(gi, carry, ls=ls, tabs=tabs):
            rows = pl.ds(pl.multiple_of(gi * SUBLANES, SUBLANES), SUBLANES)
            xr = sre_ref[rows, ls]
            xi = sim_ref[rows, ls]
            for d in steps:
                k = (1, 2, 4).index(d)
                yr, yi = _cmul(tabs[2 * k], tabs[2 * k + 1], pltpu.roll(xr, d, 0), pltpu.roll(xi, d, 0))
                xr, xi = xr + yr, xi + yi
            if seg:
                cr, ci = s0re_ref[rows, ls], s0im_ref[rows, ls]
            else:
                cr, ci = carry
            yr, yi = _cmul(tabs[6], tabs[7], cr, ci)
            xr, xi = xr + yr, xi + yi
            sre_ref[rows, ls] = xr
            sim_ref[rows, ls] = xi
            if seg:
                return carry
            return (jnp.broadcast_to(xr[SUBLANES - 1:SUBLANES, :], (SUBLANES, S5_LW)),
                    jnp.broadcast_to(xi[SUBLANES - 1:SUBLANES, :], (SUBLANES, S5_LW)))

        if seg:
            lax.fori_loop(0, tl // SUBLANES, body, 0)
        else:
            cr, ci = lax.fori_loop(0, tl // SUBLANES, body, (car_ref[0, :, ls], car_ref[1, :, ls]))
            car_ref[0, :, ls] = cr
            car_ref[1, :, ls] = ci

    if seg:
        ore_ref[...] = sre_ref[...]
        oim_ref[...] = sim_ref[...]
    else:
        ore_ref[...] = car_ref[0, 0:1, :]
        oim_ref[...] = car_ref[1, 0:1, :]

    kout = D_MODEL // S5_NBLK
    ys = []
    for k in range(S5_NBLK):
        st = jnp.concatenate([sre_ref[:, k * kst:(k + 1) * kst], sim_ref[:, k * kst:(k + 1) * kst]], axis=1)
        ys.append(_dot(st.astype(BF16), ccat_ref[k]))
    y = jnp.concatenate(ys, axis=1) + d_ref[...] * u
    g_ref[...] = _gelu_tanh(y).astype(BF16)


def _s5_scan(x, w, tl, s0=None):
    n = x.shape[0]
    seg = 0 if s0 is None else 4
    row = lambda width: pl.BlockSpec((tl, width), lambda i: (i, 0))
    wargs = (w["w_in"], w["b_re"], w["b_im"], w["a_re"], w["a_im"], w["c_cat"], w["d"])
    in_specs = [row(D_MODEL)] + [_const_spec(a.shape) for a in wargs]
    args = (x,) + wargs
    scratch = [pltpu.VMEM((tl, N_STATE), F32), pltpu.VMEM((tl, N_STATE), F32),
               pltpu.VMEM((8, SUBLANES, N_STATE), F32)]
    if seg:
        in_specs += [row(N_STATE), row(N_STATE)]
        args += tuple(s0)
        st_shape = jax.ShapeDtypeStruct((n, N_STATE), F32)
        st_spec = row(N_STATE)
    else:
        scratch.append(pltpu.VMEM((2, SUBLANES, N_STATE), F32))
        st_shape = jax.ShapeDtypeStruct((1, N_STATE), F32)
        st_spec = _const_spec((1, N_STATE))
    return pl.pallas_call(
        functools.partial(_s5_scan_kernel, seg), grid=(n // tl,), in_specs=in_specs,
        out_specs=[row(D_MODEL), st_spec, st_spec],
        out_shape=[jax.ShapeDtypeStruct((n, D_MODEL), BF16), st_shape, st_shape],
        scratch_shapes=scratch,
        compiler_params=_cparams(("arbitrary",)), name="s5_scan_seg" if seg else "s5_scan",
    )(*args)


def _glu_out_kernel(gin_ref, x_ref, wglu_ref, wo_ref, g_ref, b_ref, o_ref):
    ga = _dot(gin_ref[...], wglu_ref[...])
    hid = ga[:, :D_MODEL] * jax.nn.sigmoid(ga[:, D_MODEL:])
    z = ALPHA * x_ref[...] + _dot(hid.astype(BF16), wo_ref[...])
    o_ref[...] = _layer_norm(z, g_ref[...], b_ref[...])


def _glu_out(gin, x, w_glu, w_o, g, b, tn):
    n = x.shape[0]
    row = lambda: pl.BlockSpec((tn, D_MODEL), lambda i: (i, 0))
    return pl.pallas_call(
        _glu_out_kernel, grid=(n // tn,),
        in_specs=[row(), row(), _const_spec(w_glu.shape), _const_spec(w_o.shape),
                  _const_spec(g.shape), _const_spec(b.shape)],
        out_specs=row(), out_shape=jax.ShapeDtypeStruct((n, D_MODEL), F32),
        compiler_params=_cparams(("arbitrary",)), name="s5_glu_out",
    )(gin, x, w_glu, w_o, g, b)


def _tile(n, pref):
    return pref if n % pref == 0 else n


def kernel(x_prompt, x_sample, cache_kv_latent, cache_k_rope, state_ssm_re, state_ssm_im, page_table, mla_w_in, mla_g_q, mla_g_kv, mla_w_uq, mla_w_uk, mla_w_uv, mla_w_o, ssm_w_in, ssm_a_re, ssm_a_im, ssm_log_dt, ssm_b_re, ssm_b_im, ssm_c_re, ssm_c_im, ssm_d, ssm_w_glu, ssm_w_o, peer_w_q, peer_sub_keys, peer_u, peer_v, ln_g, ln_b):
    bp, lp, _ = x_prompt.shape
    bd, td, _ = x_sample.shape
    assert bp == 1 and td * N_HEADS == ROWS
    npr, nsm = bp * lp, bd * td
    yp = x_prompt.reshape(npr, D_MODEL)
    ys = x_sample.reshape(nsm, D_MODEL)
    lng = lambda layer, k: ln_g[layer, k].reshape(1, D_MODEL)
    lnb = lambda layer, k: ln_b[layer, k].reshape(1, D_MODEL)

    outs = {}
    for layer in range(DEPTH):
        j = layer // 2
        if layer % 2 == 0:
            w = _mla_weights(mla_w_in[j], mla_g_q[j], mla_g_kv[j], mla_w_uq[j], mla_w_uk[j], mla_w_uv[j], mla_w_o[j])
            cs_p = _rope_cs(jnp.arange(lp))
            ckv_p, kr_p, qn, qr, kn, krp, v = _mla_proj(yp, cs_p, w, False, _tile(npr, 512))
            yp = _flash_prompt(qn, qr, kn, krp, v, yp, w["w_o"], lng(layer, 0), lnb(layer, 0), _tile(npr, 512))
            cs_s = _rope_cs(jnp.tile(PAST_LEN + jnp.arange(td), bd))
            ckv_s, kr_s, qlat, qr_s = _mla_proj(ys, cs_s, w, True, nsm)
            o_lat = _decode(qlat.reshape(bd, ROWS, KV_LORA), qr_s.reshape(bd, ROWS, LANES),
                            ckv_s.reshape(bd, td, KV_LORA), kr_s.reshape(bd, td, D_ROPE),
                            cache_kv_latent[j:j + 1], cache_k_rope[j:j + 1], page_table)
            ys = _sample_out(o_lat.reshape(nsm, N_HEADS * KV_LORA), ys, w["w_uv"], w["w_o"],
                             lng(layer, 0), lnb(layer, 0))
            outs.setdefault("p_lat", []).append(ckv_p.reshape(bp, lp, KV_LORA))
            outs.setdefault("p_kr", []).append(kr_p.reshape(bp, lp, D_ROPE))
            outs.setdefault("s_lat", []).append(ckv_s.reshape(bd, td, KV_LORA))
            outs.setdefault("s_kr", []).append(kr_s.reshape(bd, td, D_ROPE))
        else:
            w = _s5_weights(ssm_w_in[j], ssm_a_re[j], ssm_a_im[j], ssm_log_dt[j], ssm_b_re[j], ssm_b_im[j],
                            ssm_c_re[j], ssm_c_im[j], ssm_d[j], ssm_w_glu[j], ssm_w_o[j])
            g_p, sre_p, sim_p = _s5_scan(yp, w, _tile(npr, 256))
            yp = _glu_out(g_p, yp, w["w_glu"], w["w_o"], lng(layer, 0), lnb(layer, 0), _tile(npr, 512))
            s0 = (jnp.repeat(state_ssm_re[j].reshape(bd, N_STATE), td, axis=0),
                  jnp.repeat(state_ssm_im[j].reshape(bd, N_STATE), td, axis=0))
            g_s, sre_s, sim_s = _s5_scan(ys, w, _tile(nsm, 128), s0)
            ys = _glu_out(g_s, ys, w["w_glu"], w["w_o"], lng(layer, 0), lnb(layer, 0), _tile(nsm, 512))
            outs.setdefault("p_sre", []).append(sre_p.reshape(bp, N_GROUPS, STATE))
            outs.setdefault("p_sim", []).append(sim_p.reshape(bp, N_GROUPS, STATE))
            outs.setdefault("s_sre", []).append(sre_s[td - 1::td].reshape(bd, N_GROUPS, STATE))
            outs.setdefault("s_sim", []).append(sim_s[td - 1::td].reshape(bd, N_GROUPS, STATE))
        pw = _peer_weights(peer_w_q[layer], peer_sub_keys[layer], peer_u[layer], peer_v[layer])
        yp = _peer_ln(yp, pw, lng(layer, 1), lnb(layer, 1), _tile(npr, 256), _tile(npr, 512))
        ys = _peer_ln(ys, pw, lng(layer, 1), lnb(layer, 1), _tile(nsm, 256), _tile(nsm, 512))

    return (yp.reshape(bp, lp, D_MODEL), ys.reshape(bd, td, D_MODEL),
            jnp.stack(outs["p_lat"]), jnp.stack(outs["p_kr"]), jnp.stack(outs["p_sre"]), jnp.stack(outs["p_sim"]),
            jnp.stack(outs["s_lat"]), jnp.stack(outs["s_kr"]), jnp.stack(outs["s_sre"]), jnp.stack(outs["s_sim"]))
```

```python
import functools
import math

import jax
import jax.numpy as jnp
from jax import lax
from jax.experimental import pallas as pl
from jax.experimental.pallas import tpu as pltpu

F32 = jnp.float32
BF16 = jnp.bfloat16

D_MODEL = 1024
DEPTH = 2
PAST_LEN = 16384
PAGE_SIZE = 128

N_HEADS = 8
D_NOPE = 128
D_ROPE = 64
D_V = 128
Q_LORA = 384
KV_LORA = 256
ROPE_THETA = 10000.0
ATTN_SCALE = 1.0 / math.sqrt(D_NOPE + D_ROPE)

GROUP_SIZE = 16
N_GROUPS = D_MODEL // GROUP_SIZE
STATE = 64
N_STATE = N_GROUPS * STATE

PEER_HEADS = 8
N_KEYS = 128
N_EXPERTS = N_KEYS * N_KEYS
HALF_KEY = 128
PEER_TOPK = 16

ALPHA = (2 * DEPTH) ** 0.25
LN_EPS = 1e-5
RMS_EPS = 1e-6

LANES = 128
SUBLANES = 8
VMEM_LIMIT = 56 * 1024 * 1024

NEG_BIG = -1e30


def _cparams(sem):
    return pltpu.CompilerParams(dimension_semantics=sem, vmem_limit_bytes=VMEM_LIMIT)


def _dot(a, b):
    return jnp.dot(a, b, preferred_element_type=F32)


def _dot_nt(a, b):
    return lax.dot_general(a, b, (((1,), (1,)), ((), ())), preferred_element_type=F32)


def _layer_norm(z, g, b):
    mu = jnp.mean(z, -1, keepdims=True)
    zc = z - mu
    var = jnp.mean(zc * zc, -1, keepdims=True)
    return zc * lax.rsqrt(var + LN_EPS) * g + b


def _rms_norm(x, g):
    return x * lax.rsqrt(jnp.mean(x * x, -1, keepdims=True) + RMS_EPS) * g


def _gelu_tanh(x):
    c = math.sqrt(2.0 / math.pi)
    return x * (0.5 * (1.0 + jnp.tanh(c * (x + 0.044715 * (x * x * x)))))


def _const_spec(shape):
    n = len(shape)
    return pl.BlockSpec(shape, lambda *_: (0,) * n)


def _mla_proj_kernel(absorb, x_ref, wcq_ref, wckv_ref, wkr_ref, gq_ref, gkv_ref,
                     wuqn_ref, wuqr_ref, wuqs_ref, cs_ref, wa_ref, wb_ref, *out_refs):
    xb = x_ref[...].astype(BF16)
    c_q = _rms_norm(_dot(xb, wcq_ref[...]), gq_ref[...])
    c_kv = _rms_norm(_dot(xb, wckv_ref[...]), gkv_ref[...])
    hk = _dot(xb, wkr_ref[...])
    cos = cs_ref[:, :LANES]
    sin = cs_ref[:, LANES:]
    kr = hk[:, :LANES] * cos + hk[:, LANES:] * sin
    cqb = c_q.astype(BF16)
    cos8 = jnp.concatenate([cos] * N_HEADS, axis=1)
    sin8 = jnp.concatenate([sin] * N_HEADS, axis=1)
    qscale = ATTN_SCALE if absorb else ATTN_SCALE * math.log2(math.e)
    qn = _dot(cqb, wuqn_ref[...]) * qscale
    qr = (_dot(cqb, wuqr_ref[...]) * cos8 + _dot(cqb, wuqs_ref[...]) * sin8) * qscale
    ckvb = c_kv.astype(BF16)
    if absorb:
        ckv_ref, kr_ref, qlat_ref, qr_ref = out_refs
        qnb = qn.astype(BF16)
        for h in range(N_HEADS):
            qlat_ref[:, h * KV_LORA:(h + 1) * KV_LORA] = _dot(
                qnb[:, h * D_NOPE:(h + 1) * D_NOPE], wa_ref[h * D_NOPE:(h + 1) * D_NOPE, :]).astype(BF16)
    else:
        ckv_ref, kr_ref, qn_ref, qr_ref, kn_ref, krp_ref, v_ref = out_refs
        qn_ref[...] = qn.astype(BF16)
        kn_ref[...] = _dot(ckvb, wa_ref[...]).astype(BF16)
        v_ref[...] = _dot(ckvb, wb_ref[...]).astype(BF16)
        krp_ref[...] = kr.astype(BF16)
    ckv_ref[...] = c_kv
    kr_ref[...] = kr[:, :D_ROPE]
    qr_ref[...] = qr.astype(BF16)


def _rope_cs(pos):
    inv = 1.0 / (ROPE_THETA ** (jnp.arange(0, D_ROPE, 2, dtype=F32) / D_ROPE))
    ang = pos.astype(F32)[:, None] * inv[None, :]
    cos, sin = jnp.cos(ang), jnp.sin(ang)
    z = jnp.zeros((pos.shape[0], LANES - D_ROPE), F32)
    return jnp.concatenate([cos, cos, z, -sin, sin, z], axis=1)


def _mla_weights(w_in, g_q, g_kv, w_uq, w_uk, w_uv, w_o):
    half = D_ROPE // 2
    w_cq = w_in[:, :Q_LORA].astype(BF16)
    w_ckv = w_in[:, Q_LORA:Q_LORA + KV_LORA].astype(BF16)
    w_k = w_in[:, Q_LORA + KV_LORA:]
    zk = jnp.zeros((D_MODEL, LANES - D_ROPE), F32)
    w_kr = jnp.concatenate([w_k, zk, w_k[:, half:], w_k[:, :half], zk], axis=1).astype(BF16)
    wq = w_uq.reshape(Q_LORA, N_HEADS, D_NOPE + D_ROPE)
    wq_n = wq[:, :, :D_NOPE].reshape(Q_LORA, N_HEADS * D_NOPE).astype(BF16)
    r = wq[:, :, D_NOPE:]
    zq = jnp.zeros((Q_LORA, N_HEADS, LANES - D_ROPE), F32)
    wq_r = jnp.concatenate([r, zq], axis=2).reshape(Q_LORA, N_HEADS * LANES).astype(BF16)
    wq_s = jnp.concatenate([r[:, :, half:], r[:, :, :half], zq], axis=2).reshape(Q_LORA, N_HEADS * LANES).astype(BF16)
    return dict(w_cq=w_cq, w_ckv=w_ckv, w_kr=w_kr, g_q=g_q.reshape(1, -1), g_kv=g_kv.reshape(1, -1),
                wq_n=wq_n, wq_r=wq_r, wq_s=wq_s, w_uk=w_uk.astype(BF16), w_ukT=w_uk.T.astype(BF16),
                w_uv=w_uv.astype(BF16), w_o=w_o.astype(BF16))


def _mla_proj(x, cs, w, absorb, tn):
    n = x.shape[0]
    hw = N_HEADS * LANES
    row = lambda width: pl.BlockSpec((tn, width), lambda i: (i, 0))
    wa = w["w_ukT"] if absorb else w["w_uk"]
    wb = w["w_uv"]
    in_specs = [row(D_MODEL), _const_spec(w["w_cq"].shape), _const_spec(w["w_ckv"].shape),
                _const_spec(w["w_kr"].shape), _const_spec(w["g_q"].shape), _const_spec(w["g_kv"].shape),
                _const_spec(w["wq_n"].shape), _const_spec(w["wq_r"].shape), _const_spec(w["wq_s"].shape),
                row(2 * LANES), _const_spec(wa.shape), _const_spec(wb.shape)]
    if absorb:
        out_shape = [jax.ShapeDtypeStruct((n, KV_LORA), F32), jax.ShapeDtypeStruct((n, D_ROPE), F32),
                     jax.ShapeDtypeStruct((n, N_HEADS * KV_LORA), BF16), jax.ShapeDtypeStruct((n, hw), BF16)]
        out_specs = [row(KV_LORA), row(D_ROPE), row(N_HEADS * KV_LORA), row(hw)]
    else:
        out_shape = [jax.ShapeDtypeStruct((n, KV_LORA), F32), jax.ShapeDtypeStruct((n, D_ROPE), F32),
                     jax.ShapeDtypeStruct((n, hw), BF16), jax.ShapeDtypeStruct((n, hw), BF16),
                     jax.ShapeDtypeStruct((n, hw), BF16), jax.ShapeDtypeStruct((n, LANES), BF16),
                     jax.ShapeDtypeStruct((n, hw), BF16)]
        out_specs = [row(KV_LORA), row(D_ROPE), row(hw), row(hw), row(hw), row(LANES), row(hw)]
    return pl.pallas_call(
        functools.partial(_mla_proj_kernel, absorb),
        grid=(n // tn,), in_specs=in_specs, out_specs=out_specs, out_shape=out_shape,
        compiler_params=_cparams(("arbitrary",)), name="mla_proj_absorb" if absorb else "mla_proj",
    )(x, w["w_cq"], w["w_ckv"], w["w_kr"], w["g_q"], w["g_kv"], w["wq_n"], w["wq_r"], w["wq_s"], cs, wa, wb)


def _flash_kernel(qn_ref, qr_ref, kn_ref, kr_ref, v_ref, x_ref, wo_ref, g_ref, b_ref, o_ref,
                  m_ref, acc_ref):
    qi = pl.program_id(0)
    ki = pl.program_id(1)
    tq = qn_ref.shape[0]
    tk = kn_ref.shape[0]

    @pl.when(ki == 0)
    def _():
        m_ref[...] = jnp.full(m_ref.shape, NEG_BIG, F32)
        acc_ref[...] = jnp.zeros(acc_ref.shape, F32)

    def step(masked):
        kr = kr_ref[...]
        ones = jnp.ones((tk, LANES), BF16)
        if masked:
            keep = (lax.broadcasted_iota(jnp.int32, (tq, tk), 1)
                    <= lax.broadcasted_iota(jnp.int32, (tq, tk), 0))

        def scores(h):
            hs = slice(h * LANES, (h + 1) * LANES)
            q = jnp.concatenate([qn_ref[:, hs], qr_ref[:, hs]], axis=1)
            k = jnp.concatenate([kn_ref[:, hs], kr], axis=1)
            return _dot_nt(q, k)

        s_next = scores(0)
        for h in range(N_HEADS):
            s = s_next
            if h + 1 < N_HEADS:
                s_next = scores(h + 1)
            if masked:
                s = jnp.where(keep, s, -jnp.inf)
            hs = slice(h * LANES, (h + 1) * LANES)
            m_prev = m_ref[h]
            m_new = jnp.maximum(m_prev, jnp.max(s, -1, keepdims=True))
            p = jnp.exp2(s - jnp.concatenate([m_new] * (tk // LANES), axis=1))
            corr = jnp.exp2(m_prev - m_new)
            v1 = jnp.concatenate([v_ref[:, hs], ones], axis=1)
            acc_ref[h] = jnp.concatenate([corr, corr], axis=1) * acc_ref[h] + _dot(p.astype(BF16), v1)
            m_ref[h] = m_new

    @pl.when(ki < qi)
    def _():
        step(False)

    @pl.when(ki == qi)
    def _():
        step(True)
        o = jnp.concatenate([acc_ref[h, :, :D_V] / acc_ref[h, :, D_V:] for h in range(N_HEADS)], axis=1)
        z = ALPHA * x_ref[...] + _dot(o.astype(BF16), wo_ref[...])
        o_ref[...] = _layer_norm(z, g_ref[...], b_ref[...])


def _flash_prompt(qn, qr, kn, krp, v, x, w_o, g, b, tq):
    n = x.shape[0]
    hw = N_HEADS * LANES
    qspec = lambda width: pl.BlockSpec((tq, width), lambda i, j: (i, 0))
    kspec = lambda width: pl.BlockSpec((tq, width), lambda i, j: (jnp.minimum(i, j), 0))
    return pl.pallas_call(
        _flash_kernel,
        grid=(n // tq, n // tq),
        in_specs=[qspec(hw), qspec(hw), kspec(hw), kspec(LANES), kspec(hw), qspec(D_MODEL),
                  _const_spec(w_o.shape), _const_spec(g.shape), _const_spec(b.shape)],
        out_specs=qspec(D_MODEL),
        out_shape=jax.ShapeDtypeStruct((n, D_MODEL), F32),
        scratch_shapes=[pltpu.VMEM((N_HEADS, tq, LANES), F32), pltpu.VMEM((N_HEADS, tq, D_V + LANES), F32)],
        compiler_params=_cparams(("arbitrary", "arbitrary")), name="mla_flash_prompt",
    )(qn, qr, kn, krp, v, x, w_o, g, b)


PAGES_PER_STEP = 16
ROWS = 32


def _decode_kernel(pt_ref, ql_ref, qr_ref, cn_ref, rn_ref, *rest):
    lat_refs = rest[:PAGES_PER_STEP]
    rope_refs = rest[PAGES_PER_STEP:2 * PAGES_PER_STEP]
    o_ref, m_ref, l_ref, acc_ref, kcat_ref, rcat_ref = rest[2 * PAGES_PER_STEP:]
    c = pl.program_id(1)
    nc = pl.num_programs(1)

    @pl.when(c == 0)
    def _():
        m_ref[...] = jnp.full(m_ref.shape, NEG_BIG, F32)
        l_ref[...] = jnp.zeros(l_ref.shape, F32)
        acc_ref[...] = jnp.zeros(acc_ref.shape, F32)

    ql = ql_ref[...]
    qr = qr_ref[:, :D_ROPE]
    for p in range(PAGES_PER_STEP):
        kcat_ref[p * PAGE_SIZE:(p + 1) * PAGE_SIZE, :] = lat_refs[p][...].astype(BF16)
        rcat_ref[:, p * PAGE_SIZE:(p + 1) * PAGE_SIZE] = rope_refs[p][...].astype(BF16)
    kcat = kcat_ref[...]
    s = _dot_nt(ql, kcat) + _dot(qr, rcat_ref[...])
    m_prev = m_ref[...]
    m_new = jnp.maximum(m_prev, jnp.max(s, -1, keepdims=True))
    corr = jnp.exp(m_prev - m_new)
    pp = jnp.exp(s - m_new)
    m_ref[...] = m_new
    l_ref[...] = corr * l_ref[...] + jnp.sum(pp, -1, keepdims=True)
    acc_ref[...] = corr * acc_ref[...] + _dot(pp.astype(BF16), kcat)

    @pl.when(c == nc - 1)
    def _():
        qlf = ql.astype(F32)
        qrf = qr.astype(F32)
        cn = cn_ref[...].astype(BF16).astype(F32)
        rn = rn_ref[...].astype(BF16).astype(F32)
        t_of_row = lax.broadcasted_iota(jnp.int32, (ROWS, 1), 0) // N_HEADS
        n_new = cn.shape[0]
        s_new = []
        for j in range(n_new):
            sj = (jnp.sum(qlf * cn[j:j + 1, :], -1, keepdims=True)
                  + jnp.sum(qrf * rn[j:j + 1, :], -1, keepdims=True))
            s_new.append(jnp.where(j <= t_of_row, sj, -jnp.inf))
        m0 = m_ref[...]
        m1 = m0
        for sj in s_new:
            m1 = jnp.maximum(m1, sj)
        corr1 = jnp.exp(m0 - m1)
        l1 = corr1 * l_ref[...]
        a1 = corr1 * acc_ref[...]
        for j in range(n_new):
            pj = jnp.exp(s_new[j] - m1)
            l1 = l1 + pj
            a1 = a1 + pj.astype(BF16).astype(F32) * cn[j:j + 1, :]
        o_ref[...] = a1 / l1


def _decode(qlat, qr, ckv_new, kr_new, cache_lat, cache_kr, page_table):
    nb, n_pages = page_table.shape
    nc = n_pages // PAGES_PER_STEP
    n_new = ckv_new.shape[1]

    def page_spec(p, rows, width):
        return pl.BlockSpec((None, None, rows, width),
                            lambda b, c, pt: (0, pt[b, c * PAGES_PER_STEP + p], 0, 0))

    per_b = lambda rows, width: pl.BlockSpec((None, rows, width), lambda b, c, pt: (b, 0, 0))
    in_specs = ([per_b(ROWS, KV_LORA), per_b(ROWS, LANES), per_b(n_new, KV_LORA), per_b(n_new, D_ROPE)]
                + [page_spec(p, PAGE_SIZE, KV_LORA) for p in range(PAGES_PER_STEP)]
                + [page_spec(p, D_ROPE, PAGE_SIZE) for p in range(PAGES_PER_STEP)])
    keys = PAGES_PER_STEP * PAGE_SIZE
    grid_spec = pltpu.PrefetchScalarGridSpec(
        num_scalar_prefetch=1, grid=(nb, nc), in_specs=in_specs,
        out_specs=per_b(ROWS, KV_LORA),
        scratch_shapes=[pltpu.VMEM((ROWS, 1), F32), pltpu.VMEM((ROWS, 1), F32), pltpu.VMEM((ROWS, KV_LORA), F32),
                        pltpu.VMEM((keys, KV_LORA), BF16), pltpu.VMEM((D_ROPE, keys), BF16)])
    return pl.pallas_call(
        _decode_kernel, grid_spec=grid_spec,
        out_shape=jax.ShapeDtypeStruct((nb, ROWS, KV_LORA), F32),
        compiler_params=_cparams(("arbitrary", "arbitrary")), name="mla_decode",
    )(page_table, qlat, qr, ckv_new, kr_new, *([cache_lat] * PAGES_PER_STEP), *([cache_kr] * PAGES_PER_STEP))


def _sample_out_kernel(ol_ref, x_ref, wuv_ref, wo_ref, g_ref, b_ref, o_ref):
    olb = ol_ref[...].astype(BF16)
    o = jnp.concatenate(
        [_dot(olb[:, h * KV_LORA:(h + 1) * KV_LORA], wuv_ref[:, h * D_V:(h + 1) * D_V]) for h in range(N_HEADS)],
        axis=1)
    z = ALPHA * x_ref[...] + _dot(o.astype(BF16), wo_ref[...])
    o_ref[...] = _layer_norm(z, g_ref[...], b_ref[...])


def _sample_out(o_lat, x, w_uv, w_o, g, b):
    n = x.shape[0]
    args = (o_lat, x, w_uv, w_o, g, b)
    return pl.pallas_call(
        _sample_out_kernel, grid=(1,), in_specs=[_const_spec(a.shape) for a in args],
        out_specs=_const_spec((n, D_MODEL)), out_shape=jax.ShapeDtypeStruct((n, D_MODEL), F32),
        compiler_params=_cparams(("arbitrary",)), name="mla_sample_out",
    )(*args)


def _sort16_network():
    def merge(lo, hi, r):
        step = r * 2
        if step < hi - lo:
            yield from merge(lo, hi, step)
            yield from merge(lo + r, hi, step)
            yield from [(i, i + r) for i in range(lo + r, hi - r, step)]
        else:
            yield (lo, lo + r)

    def sort(lo, hi):
        if hi - lo >= 1:
            mid = lo + (hi - lo) // 2
            yield from sort(lo, mid)
            yield from sort(mid + 1, hi)
            yield from merge(lo, hi, 1)

    return tuple(sort(0, PEER_TOPK - 1))


_SORT16 = _sort16_network()


def _sort16_desc(v):
    v = list(v)
    for i, j in _SORT16:
        v[i], v[j] = jnp.maximum(v[i], v[j]), jnp.minimum(v[i], v[j])
    return v


def _bitonic_merge16_desc(v):
    v = list(v)
    d = PEER_TOPK // 2
    while d >= 1:
        for i in range(PEER_TOPK):
            if i & d == 0:
                v[i], v[i + d] = jnp.maximum(v[i], v[i + d]), jnp.minimum(v[i], v[i + d])
        d //= 2
    return v


def _merge_top16(x, y):
    return _bitonic_merge16_desc([jnp.maximum(x[k], y[PEER_TOPK - 1 - k]) for k in range(PEER_TOPK)])


def _top16_rows(s):
    v = _sort16_desc([s[SUBLANES * k:SUBLANES * (k + 1), :] for k in range(N_KEYS // SUBLANES)])
    for shift in (4, 2, 1):
        v = _merge_top16(v, [pltpu.roll(a, shift, 0) for a in v])
    return v


def _pair_region():
    return [(a, b) for a in range(PEER_TOPK) for b in range(PEER_TOPK) if (a + 1) * (b + 1) <= PEER_TOPK]


def _peer_route_kernel(x_ref, wqt_ref, keys_ref, s1_ref, e1_ref, s2_ref, e2_ref, tau_ref):
    tn = x_ref.shape[0]
    xb = x_ref[...].astype(BF16)
    qt = _dot_nt(wqt_ref[...], xb)
    sub = lax.broadcasted_iota(jnp.int32, (SUBLANES, tn), 0)
    tops = []
    packed = [[jnp.zeros((SUBLANES, tn), F32) for _ in range(PEER_TOPK)] for _ in range(2)]
    scores = []
    for h in range(PEER_HEADS):
        for half in range(2):
            c = 2 * h + half
            st = _dot(keys_ref[c], qt[c * HALF_KEY:(c + 1) * HALF_KEY, :].astype(BF16))
            scores.append(st)
            top = _top16_rows(st)
            tops.append(top[0])
            for k in range(PEER_TOPK):
                packed[half][k] = jnp.where(sub == h, top[k], packed[half][k])
    cands = [packed[0][a] + packed[1][b] for a, b in _pair_region()]
    pad = (-len(cands)) % PEER_TOPK
    cands += [jnp.full((SUBLANES, tn), -jnp.inf, F32)] * pad
    best = _sort16_desc(cands[:PEER_TOPK])
    for g in range(1, len(cands) // PEER_TOPK):
        best = _merge_top16(best, _sort16_desc(cands[g * PEER_TOPK:(g + 1) * PEER_TOPK]))
    tau = best[PEER_TOPK - 1]
    z = jnp.zeros((SUBLANES, tn), F32)
    for k in range(PEER_TOPK):
        z = z + jnp.exp(best[k] - best[0])
    inv_z = 1.0 / z
    tau_ref[...] = tau
    for h in range(PEER_HEADS):
        m1 = jnp.concatenate([tops[2 * h]] * (N_KEYS // SUBLANES), axis=0)
        m2 = jnp.concatenate([tops[2 * h + 1]] * (N_KEYS // SUBLANES), axis=0)
        s1_ref[h] = scores[2 * h]
        s2_ref[h] = scores[2 * h + 1]
        e1_ref[h] = jnp.exp(scores[2 * h] - m1) * inv_z[h:h + 1, :]
        e2_ref[h] = jnp.exp(scores[2 * h + 1] - m2)


def _peer_route(x, w_qt, keys, tn):
    n = x.shape[0]
    big = jax.ShapeDtypeStruct((PEER_HEADS, N_KEYS, n), F32)
    bspec = pl.BlockSpec((PEER_HEADS, N_KEYS, tn), lambda i: (0, 0, i))
    return pl.pallas_call(
        _peer_route_kernel, grid=(n // tn,),
        in_specs=[pl.BlockSpec((tn, D_MODEL), lambda i: (i, 0)), _const_spec(w_qt.shape), _const_spec(keys.shape)],
        out_specs=[bspec, bspec, bspec, bspec, pl.BlockSpec((PEER_HEADS, tn), lambda i: (0, i))],
        out_shape=[big, big, big, big, jax.ShapeDtypeStruct((PEER_HEADS, n), F32)],
        compiler_params=_cparams(("arbitrary",)), name="peer_route",
    )(x, w_qt, keys)


PEER_TE = SUBLANES * N_KEYS
PEER_NCHUNK = 8
PEER_JB = 2


def _peer_dense_kernel(x_ref, u_ref, vt_ref, s1_ref, e1_ref, s2_ref, e2_ref, tau_ref, g_ref, b_ref,
                       o_ref, acc_ref, h_ref, gate_ref, wb_ref, xb_ref, s1b_ref, e1b_ref, taub_ref):
    j = pl.program_id(1)
    last = pl.num_programs(1) - 1
    tm = x_ref.shape[0]
    slot = j % 2
    prev = 1 - slot
    dc = D_MODEL // PEER_NCHUNK

    def weighted_prev(rows):
        return (_gelu_tanh(h_ref[prev, rows, :]) * gate_ref[prev, rows, :]).astype(BF16)

    @pl.when(j == 0)
    def _():
        acc_ref[...] = jnp.zeros(acc_ref.shape, F32)
        h_ref[1] = jnp.zeros(h_ref.shape[1:], F32)
        gate_ref[1] = jnp.zeros(gate_ref.shape[1:], F32)
        xb_ref[...] = x_ref[...].astype(BF16)
        for h in range(PEER_HEADS):
            taub_ref[h * SUBLANES:(h + 1) * SUBLANES, :] = jnp.broadcast_to(tau_ref[h:h + 1, :], (SUBLANES, tm))

    @pl.when(j < last)
    def _():
        h_ref[slot] = _dot_nt(u_ref[...], xb_ref[...])
        wb_ref[...] = weighted_prev(slice(0, PEER_TE))
        for h in range(PEER_HEADS):
            for ii in range(SUBLANES):
                brow = slice((h * SUBLANES + ii) * SUBLANES, (h * SUBLANES + ii + 1) * SUBLANES)
                s1b_ref[brow, :] = jnp.broadcast_to(s1_ref[h, ii:ii + 1, :], (SUBLANES, tm))
                e1b_ref[brow, :] = jnp.broadcast_to(e1_ref[h, ii:ii + 1, :], (SUBLANES, tm))

        def chunk(k, carry):
            drows = pl.ds(pl.multiple_of(k * dc, dc), dc)
            j0 = k * (PEER_JB * SUBLANES)
            nlc = tm // LANES
            zero = jnp.zeros((SUBLANES, LANES), F32)
            sixteen = jnp.uint32(16)

            def zero_after(v):
                bits = pltpu.bitcast(v, jnp.uint32)
                return pltpu.bitcast(lax.shift_right_logical(lax.shift_right_logical(bits, sixteen), sixteen), F32)

            part = _dot(vt_ref[drows, :], wb_ref[...])
            acc_ref[drows, :] += part
            for lc in range(nlc):
                ls = slice(lc * LANES, (lc + 1) * LANES)
                if lc == nlc - 1:
                    zero = zero_after(part[dc - SUBLANES:, tm - LANES:])
                gates = [[zero for _ in range(SUBLANES)] for _ in range(PEER_JB)]
                for h in range(PEER_HEADS):
                    tau = taub_ref[h * SUBLANES:(h + 1) * SUBLANES, ls]
                    jrows = [pl.ds(pl.multiple_of(j0 + t * SUBLANES, SUBLANES), SUBLANES) for t in range(PEER_JB)]
                    s2 = [s2_ref[h, jrows[t], ls] for t in range(PEER_JB)]
                    e2 = [e2_ref[h, jrows[t], ls] for t in range(PEER_JB)]
                    for ii in range(SUBLANES):
                        brow = slice((h * SUBLANES + ii) * SUBLANES, (h * SUBLANES + ii + 1) * SUBLANES)
                        s1 = s1b_ref[brow, ls]
                        e1 = e1b_ref[brow, ls]
                        for t in range(PEER_JB):
                            gates[t][ii] = gates[t][ii] + jnp.where(s2[t] + s1 >= tau, e2[t] * e1, 0.0)
                for t in range(PEER_JB):
                    for ii in range(SUBLANES):
                        r0 = pl.multiple_of(ii * N_KEYS + j0 + t * SUBLANES, SUBLANES)
                        gate_ref[slot, pl.ds(r0, SUBLANES), ls] = gates[t][ii]
            return carry

        lax.fori_loop(0, PEER_NCHUNK, chunk, 0)

    @pl.when(j == last)
    def _():
        acc = acc_ref[...] + _dot(vt_ref[...], weighted_prev(slice(0, PEER_TE)))
        z = ALPHA * x_ref[...] + acc.T
        o_ref[...] = _layer_norm(z, g_ref[...], b_ref[...])


def _peer_dense(x, u, vt, s1, e1, s2, e2, tau, g, b, tm):
    n = x.shape[0]
    nt = N_EXPERTS // PEER_TE
    full = pl.BlockSpec((PEER_HEADS, N_KEYS, tm), lambda i, j: (0, 0, i))
    part = pl.BlockSpec((PEER_HEADS, SUBLANES, tm), lambda i, j: (0, jnp.minimum(j, nt - 1), i))
    nb = PEER_HEADS * SUBLANES * SUBLANES
    return pl.pallas_call(
        _peer_dense_kernel, grid=(n // tm, nt + 1),
        in_specs=[pl.BlockSpec((tm, D_MODEL), lambda i, j: (i, 0)),
                  pl.BlockSpec((PEER_TE, D_MODEL), lambda i, j: (jnp.minimum(j, nt - 1), 0)),
                  pl.BlockSpec((D_MODEL, PEER_TE), lambda i, j: (0, jnp.maximum(j - 1, 0))),
                  part, part, full, full,
                  pl.BlockSpec((PEER_HEADS, tm), lambda i, j: (0, i)),
                  _const_spec(g.shape), _const_spec(b.shape)],
        out_specs=pl.BlockSpec((tm, D_MODEL), lambda i, j: (i, 0)),
        out_shape=jax.ShapeDtypeStruct((n, D_MODEL), F32),
        scratch_shapes=[pltpu.VMEM((D_MODEL, tm), F32), pltpu.VMEM((2, PEER_TE, tm), F32),
                        pltpu.VMEM((2, PEER_TE, tm), F32), pltpu.VMEM((PEER_TE, tm), BF16),
                        pltpu.VMEM((tm, D_MODEL), BF16),
                        pltpu.VMEM((nb, tm), F32), pltpu.VMEM((nb, tm), F32),
                        pltpu.VMEM((PEER_HEADS * SUBLANES, tm), F32)],
        compiler_params=_cparams(("arbitrary", "arbitrary")), name="peer_dense",
    )(x, u, vt, s1, e1, s2, e2, tau, g, b)


def _peer_weights(w_q, sub_keys, exp_u, exp_v):
    return dict(w_qt=w_q.T.astype(BF16),
                keys=sub_keys.reshape(2 * PEER_HEADS, N_KEYS, HALF_KEY).astype(BF16),
                u=exp_u.astype(BF16), vt=exp_v.T.astype(BF16))


def _peer_ln(x, w, g, b, tn_route, tm):
    s1, e1, s2, e2, tau = _peer_route(x, w["w_qt"], w["keys"], tn_route)
    return _peer_dense(x, w["u"], w["vt"], s1, e1, s2, e2, tau, g, b, tm)


def _s5_disc_kernel(are_ref, aim_ref, ldt_ref, bre_ref, bim_ref, abre_ref, abim_ref, bbre_ref, bbim_ref):
    lr = jnp.minimum(are_ref[...], -1e-4)
    li = aim_ref[...]
    dt = jnp.exp(ldt_ref[...])
    mag = jnp.exp(lr * dt)
    ab_re = mag * jnp.cos(li * dt)
    ab_im = mag * jnp.sin(li * dt)
    den = lr * lr + li * li
    nr, ni = ab_re - 1.0, ab_im
    f_re = (nr * lr + ni * li) / den
    f_im = (ni * lr - nr * li) / den
    br, bi = bre_ref[...], bim_ref[...]
    abre_ref[...] = ab_re
    abim_ref[...] = ab_im
    bbre_ref[...] = f_re * br - f_im * bi
    bbim_ref[...] = f_re * bi + f_im * br


def _s5_weights(w_in, a_re, a_im, log_dt, b_re, b_im, c_re, c_im, d_skip, w_glu, w_o):
    rep = lambda a: jnp.repeat(a, GROUP_SIZE, axis=0)
    bt = lambda b: b.transpose(0, 2, 1).reshape(D_MODEL, STATE)
    args = (rep(a_re), rep(a_im), rep(log_dt.reshape(N_GROUPS, 1)), bt(b_re), bt(b_im))
    sd = jax.ShapeDtypeStruct((D_MODEL, STATE), F32)
    ab_re, ab_im, bb_re, bb_im = pl.pallas_call(
        _s5_disc_kernel, grid=(1,), in_specs=[_const_spec(a.shape) for a in args],
        out_specs=[_const_spec(sd.shape)] * 4, out_shape=[sd] * 4,
        compiler_params=_cparams(("arbitrary",)), name="s5_discretize")(*args)
    nblk = 4
    gl = N_GROUPS // nblk
    eye = jnp.eye(gl, dtype=F32)

    def bdiag_in(bb):
        t = bb.reshape(nblk, gl, GROUP_SIZE, STATE)
        return jnp.einsum('kgpn,gh->kgphn', t, eye).reshape(nblk, gl * GROUP_SIZE, gl * STATE).astype(BF16)

    def bdiag_out(c):
        t = c.reshape(nblk, gl, GROUP_SIZE, STATE)
        return jnp.einsum('kgpn,gh->kgnhp', t, eye).reshape(nblk, gl * STATE, gl * GROUP_SIZE)

    c_cat = jnp.concatenate([bdiag_out(c_re), -bdiag_out(c_im)], axis=1).astype(BF16)
    return dict(w_in=w_in.astype(BF16), b_re=bdiag_in(bb_re), b_im=bdiag_in(bb_im),
                a_re=ab_re[::GROUP_SIZE].reshape(1, N_STATE), a_im=ab_im[::GROUP_SIZE].reshape(1, N_STATE),
                c_cat=c_cat, d=d_skip.reshape(1, D_MODEL), w_glu=w_glu.astype(BF16), w_o=w_o.astype(BF16))


S5_LW = 2 * LANES
S5_NBLK = 4


def _cmul(ar, ai, br, bi):
    return ar * br - ai * bi, ar * bi + ai * br


def _s5_scan_kernel(seg, x_ref, win_ref, bre_ref, bim_ref, are_ref, aim_ref, ccat_ref, d_ref, *rest):
    if seg:
        s0re_ref, s0im_ref, g_ref, ore_ref, oim_ref, sre_ref, sim_ref, tab_ref = rest
    else:
        g_ref, ore_ref, oim_ref, sre_ref, sim_ref, tab_ref, car_ref = rest
    i = pl.program_id(0)
    tl = x_ref.shape[0]
    period = seg if seg else SUBLANES

    @pl.when(i == 0)
    def _():
        ar = jnp.broadcast_to(are_ref[...], (SUBLANES, N_STATE))
        ai = jnp.broadcast_to(aim_ref[...], (SUBLANES, N_STATE))
        rr = lax.broadcasted_iota(jnp.int32, (SUBLANES, N_STATE), 0) % period
        pr, pi = ar, ai
        powers = [(pr, pi)]
        for _ in range(SUBLANES - 1):
            pr, pi = _cmul(pr, pi, ar, ai)
            powers.append((pr, pi))
        for k, d in enumerate((1, 2, 4)):
            tab_ref[2 * k] = jnp.where(rr >= d, powers[d - 1][0], 0.0)
            tab_ref[2 * k + 1] = jnp.where(rr >= d, powers[d - 1][1], 0.0)
        cr = jnp.zeros((SUBLANES, N_STATE), F32)
        ci = jnp.zeros((SUBLANES, N_STATE), F32)
        for p in range(SUBLANES):
            cr = jnp.where(rr == p, powers[p][0], cr)
            ci = jnp.where(rr == p, powers[p][1], ci)
        tab_ref[6] = cr
        tab_ref[7] = ci
        if not seg:
            car_ref[...] = jnp.zeros(car_ref.shape, F32)

    u = _dot(x_ref[...].astype(BF16), win_ref[...])
    ub = u.astype(BF16)
    kin = D_MODEL // S5_NBLK
    kst = N_STATE // S5_NBLK
    for k in range(S5_NBLK):
        sre_ref[:, k * kst:(k + 1) * kst] = _dot(ub[:, k * kin:(k + 1) * kin], bre_ref[k])
        sim_ref[:, k * kst:(k + 1) * kst] = _dot(ub[:, k * kin:(k + 1) * kin], bim_ref[k])

    steps = tuple(d for d in (1, 2, 4) if d < period)
    for lc in range(N_STATE // S5_LW):
        ls = slice(lc * S5_LW, (lc + 1) * S5_LW)
        tabs = [tab_ref[t, :, ls] for t in range(8)]

        def body(gi, carry, ls=ls, tabs=tabs):
            rows = pl.ds(pl.multiple_of(gi * SUBLANES, SUBLANES), SUBLANES)
            xr = sre_ref[rows, ls]
            xi = sim_ref[rows, ls]
            for d in steps:
                k = (1, 2, 4).index(d)
                yr, yi = _cmul(tabs[2 * k], tabs[2 * k + 1], pltpu.roll(xr, d, 0), pltpu.roll(xi, d, 0))
                xr, xi = xr + yr, xi + yi
            if seg:
                cr, ci = s0re_ref[rows, ls], s0im_ref[rows, ls]
            else:
                cr, ci = carry
            yr, yi = _cmul(tabs[6], tabs[7], cr, ci)
            xr, xi = xr + yr, xi + yi
            sre_ref[rows, ls] = xr
            sim_ref[rows, ls] = xi
            if seg:
                return carry
            return (jnp.broadcast_to(xr[SUBLANES - 1:SUBLANES, :], (SUBLANES, S5_LW)),
                    jnp.broadcast_to(xi[SUBLANES - 1:SUBLANES, :], (SUBLANES, S5_LW)))

        if seg:
            lax.fori_loop(0, tl // SUBLANES, body, 0)
        else:
            cr, ci = lax.fori_loop(0, tl // SUBLANES, body, (car_ref[0, :, ls], car_ref[1, :, ls]))
            car_ref[0, :, ls] = cr
            car_ref[1, :, ls] = ci

    if seg:
        ore_ref[...] = sre_ref[...]
        oim_ref[...] = sim_ref[...]
    else:
        ore_ref[...] = car_ref[0, 0:1, :]
        oim_ref[...] = car_ref[1, 0:1, :]

    kout = D_MODEL // S5_NBLK
    ys = []
    for k in range(S5_NBLK):
        st = jnp.concatenate([sre_ref[:, k * kst:(k + 1) * kst], sim_ref[:, k * kst:(k + 1) * kst]], axis=1)
        ys.append(_dot(st.astype(BF16), ccat_ref[k]))
    y = jnp.concatenate(ys, axis=1) + d_ref[...] * u
    g_ref[...] = _gelu_tanh(y).astype(BF16)


def _s5_scan(x, w, tl, s0=None):
    n = x.shape[0]
    seg = 0 if s0 is None else 4
    row = lambda width: pl.BlockSpec((tl, width), lambda i: (i, 0))
    wargs = (w["w_in"], w["b_re"], w["b_im"], w["a_re"], w["a_im"], w["c_cat"], w["d"])
    in_specs = [row(D_MODEL)] + [_const_spec(a.shape) for a in wargs]
    args = (x,) + wargs
    scratch = [pltpu.VMEM((tl, N_STATE), F32), pltpu.VMEM((tl, N_STATE), F32),
               pltpu.VMEM((8, SUBLANES, N_STATE), F32)]
    if seg:
        in_specs += [row(N_STATE), row(N_STATE)]
        args += tuple(s0)
        st_shape = jax.ShapeDtypeStruct((n, N_STATE), F32)
        st_spec = row(N_STATE)
    else:
        scratch.append(pltpu.VMEM((2, SUBLANES, N_STATE), F32))
        st_shape = jax.ShapeDtypeStruct((1, N_STATE), F32)
        st_spec = _const_spec((1, N_STATE))
    return pl.pallas_call(
        functools.partial(_s5_scan_kernel, seg), grid=(n // tl,), in_specs=in_specs,
        out_specs=[row(D_MODEL), st_spec, st_spec],
        out_shape=[jax.ShapeDtypeStruct((n, D_MODEL), BF16), st_shape, st_shape],
        scratch_shapes=scratch,
        compiler_params=_cparams(("arbitrary",)), name="s5_scan_seg" if seg else "s5_scan",
    )(*args)


def _glu_out_kernel(gin_ref, x_ref, wglu_ref, wo_ref, g_ref, b_ref, o_ref):
    ga = _dot(gin_ref[...], wglu_ref[...])
    hid = ga[:, :D_MODEL] * jax.nn.sigmoid(ga[:, D_MODEL:])
    z = ALPHA * x_ref[...] + _dot(hid.astype(BF16), wo_ref[...])
    o_ref[...] = _layer_norm(z, g_ref[...], b_ref[...])


def _glu_out(gin, x, w_glu, w_o, g, b, tn):
    n = x.shape[0]
    row = lambda: pl.BlockSpec((tn, D_MODEL), lambda i: (i, 0))
    return pl.pallas_call(
        _glu_out_kernel, grid=(n // tn,),
        in_specs=[row(), row(), _const_spec(w_glu.shape), _const_spec(w_o.shape),
                  _const_spec(g.shape), _const_spec(b.shape)],
        out_specs=row(), out_shape=jax.ShapeDtypeStruct((n, D_MODEL), F32),
        compiler_params=_cparams(("arbitrary",)), name="s5_glu_out",
    )(gin, x, w_glu, w_o, g, b)


def _tile(n, pref):
    return pref if n % pref == 0 else n


def kernel(x_prompt, x_sample, cache_kv_latent, cache_k_rope, state_ssm_re, state_ssm_im, page_table, mla_w_in, mla_g_q, mla_g_kv, mla_w_uq, mla_w_uk, mla_w_uv, mla_w_o, ssm_w_in, ssm_a_re, ssm_a_im, ssm_log_dt, ssm_b_re, ssm_b_im, ssm_c_re, ssm_c_im, ssm_d, ssm_w_glu, ssm_w_o, peer_w_q, peer_sub_keys, peer_u, peer_v, ln_g, ln_b):
    bp, lp, _ = x_prompt.shape
    bd, td, _ = x_sample.shape
    assert bp == 1 and td * N_HEADS == ROWS
    npr, nsm = bp * lp, bd * td
    yp = x_prompt.reshape(npr, D_MODEL)
    ys = x_sample.reshape(nsm, D_MODEL)
    lng = lambda layer, k: ln_g[layer, k].reshape(1, D_MODEL)
    lnb = lambda layer, k: ln_b[layer, k].reshape(1, D_MODEL)

    outs = {}
    for layer in range(DEPTH):
        j = layer // 2
        if layer % 2 == 0:
            w = _mla_weights(mla_w_in[j], mla_g_q[j], mla_g_kv[j], mla_w_uq[j], mla_w_uk[j], mla_w_uv[j], mla_w_o[j])
            cs_p = _rope_cs(jnp.arange(lp))
            ckv_p, kr_p, qn, qr, kn, krp, v = _mla_proj(yp, cs_p, w, False, _tile(npr, 512))
            yp = _flash_prompt(qn, qr, kn, krp, v, yp, w["w_o"], lng(layer, 0), lnb(layer, 0), _tile(npr, 512))
            cs_s = _rope_cs(jnp.tile(PAST_LEN + jnp.arange(td), bd))
            ckv_s, kr_s, qlat, qr_s = _mla_proj(ys, cs_s, w, True, nsm)
            o_lat = _decode(qlat.reshape(bd, ROWS, KV_LORA), qr_s.reshape(bd, ROWS, LANES),
                            ckv_s.reshape(bd, td, KV_LORA), kr_s.reshape(bd, td, D_ROPE),
                            cache_kv_latent[j:j + 1], jnp.swapaxes(cache_k_rope[j:j + 1], 2, 3), page_table)
            ys = _sample_out(o_lat.reshape(nsm, N_HEADS * KV_LORA), ys, w["w_uv"], w["w_o"],
                             lng(layer, 0), lnb(layer, 0))
            outs.setdefault("p_lat", []).append(ckv_p.reshape(bp, lp, KV_LORA))
            outs.setdefault("p_kr", []).append(kr_p.reshape(bp, lp, D_ROPE))
            outs.setdefault("s_lat", []).append(ckv_s.reshape(bd, td, KV_LORA))
            outs.setdefault("s_kr", []).append(kr_s.reshape(bd, td, D_ROPE))
        else:
            w = _s5_weights(ssm_w_in[j], ssm_a_re[j], ssm_a_im[j], ssm_log_dt[j], ssm_b_re[j], ssm_b_im[j],
                            ssm_c_re[j], ssm_c_im[j], ssm_d[j], ssm_w_glu[j], ssm_w_o[j])
            g_p, sre_p, sim_p = _s5_scan(yp, w, _tile(npr, 256))
            yp = _glu_out(g_p, yp, w["w_glu"], w["w_o"], lng(layer, 0), lnb(layer, 0), _tile(npr, 512))
            s0 = (jnp.repeat(state_ssm_re[j].reshape(bd, N_STATE), td, axis=0),
                  jnp.repeat(state_ssm_im[j].reshape(bd, N_STATE), td, axis=0))
            g_s, sre_s, sim_s = _s5_scan(ys, w, _tile(nsm, 128), s0)
            ys = _glu_out(g_s, ys, w["w_glu"], w["w_o"], lng(layer, 0), lnb(layer, 0), _tile(nsm, 512))
            outs.setdefault("p_sre", []).append(sre_p.reshape(bp, N_GROUPS, STATE))
            outs.setdefault("p_sim", []).append(sim_p.reshape(bp, N_GROUPS, STATE))
            outs.setdefault("s_sre", []).append(sre_s[td - 1::td].reshape(bd, N_GROUPS, STATE))
            outs.setdefault("s_sim", []).append(sim_s[td - 1::td].reshape(bd, N_GROUPS, STATE))
        pw = _peer_weights(peer_w_q[layer], peer_sub_keys[layer], peer_u[layer], peer_v[layer])
        yp = _peer_ln(yp, pw, lng(layer, 1), lnb(layer, 1), _tile(npr, 256), _tile(npr, 512))
        ys = _peer_ln(ys, pw, lng(layer, 1), lnb(layer, 1), _tile(nsm, 256), _tile(nsm, 512))

    return (yp.reshape(bp, lp, D_MODEL), ys.reshape(bd, td, D_MODEL),
            jnp.stack(outs["p_lat"]), jnp.stack(outs["p_kr"]), jnp.stack(outs["p_sre"]), jnp.stack(outs["p_sim"]),
            jnp.stack(outs["s_lat"]), jnp.stack(outs["s_kr"]), jnp.stack(outs["s_sre"]), jnp.stack(outs["s_sim"]))
```

```python
import functools
import math

import jax
import jax.numpy as jnp
from jax import lax
from jax.experimental import pallas as pl
from jax.experimental.pallas import tpu as pltpu

F32 = jnp.float32
BF16 = jnp.bfloat16

D_MODEL = 1024
DEPTH = 2
PAST_LEN = 16384
PAGE_SIZE = 128

N_HEADS = 8
D_NOPE = 128
D_ROPE = 64
D_V = 128
Q_LORA = 384
KV_LORA = 256
ROPE_THETA = 10000.0
ATTN_SCALE = 1.0 / math.sqrt(D_NOPE + D_ROPE)

GROUP_SIZE = 16
N_GROUPS = D_MODEL // GROUP_SIZE
STATE = 64
N_STATE = N_GROUPS * STATE

PEER_HEADS = 8
N_KEYS = 128
N_EXPERTS = N_KEYS * N_KEYS
HALF_KEY = 128
PEER_TOPK = 16

ALPHA = (2 * DEPTH) ** 0.25
LN_EPS = 1e-5
RMS_EPS = 1e-6

LANES = 128
SUBLANES = 8
VMEM_LIMIT = 56 * 1024 * 1024

NEG_BIG = -1e30


def _cparams(sem):
    return pltpu.CompilerParams(dimension_semantics=sem, vmem_limit_bytes=VMEM_LIMIT)


def _dot(a, b):
    return jnp.dot(a, b, preferred_element_type=F32)


def _dot_nt(a, b):
    return lax.dot_general(a, b, (((1,), (1,)), ((), ())), preferred_element_type=F32)


def _layer_norm(z, g, b):
    mu = jnp.mean(z, -1, keepdims=True)
    zc = z - mu
    var = jnp.mean(zc * zc, -1, keepdims=True)
    return zc * lax.rsqrt(var + LN_EPS) * g + b


def _rms_norm(x, g):
    return x * lax.rsqrt(jnp.mean(x * x, -1, keepdims=True) + RMS_EPS) * g


def _gelu_tanh(x):
    c = math.sqrt(2.0 / math.pi)
    return x * (0.5 * (1.0 + jnp.tanh(c * (x + 0.044715 * (x * x * x)))))


def _const_spec(shape):
    n = len(shape)
    return pl.BlockSpec(shape, lambda *_: (0,) * n)


def _mla_proj_kernel(absorb, x_ref, wcq_ref, wckv_ref, wkr_ref, gq_ref, gkv_ref,
                     wuqn_ref, wuqr_ref, wuqs_ref, cs_ref, wa_ref, wb_ref, *out_refs):
    xb = x_ref[...].astype(BF16)
    c_q = _rms_norm(_dot(xb, wcq_ref[...]), gq_ref[...])
    c_kv = _rms_norm(_dot(xb, wckv_ref[...]), gkv_ref[...])
    hk = _dot(xb, wkr_ref[...])
    cos = cs_ref[:, :LANES]
    sin = cs_ref[:, LANES:]
    kr = hk[:, :LANES] * cos + hk[:, LANES:] * sin
    cqb = c_q.astype(BF16)
    cos8 = jnp.concatenate([cos] * N_HEADS, axis=1)
    sin8 = jnp.concatenate([sin] * N_HEADS, axis=1)
    qscale = ATTN_SCALE if absorb else ATTN_SCALE * math.log2(math.e)
    qn = _dot(cqb, wuqn_ref[...]) * qscale
    qr = (_dot(cqb, wuqr_ref[...]) * cos8 + _dot(cqb, wuqs_ref[...]) * sin8) * qscale
    ckvb = c_kv.astype(BF16)
    if absorb:
        ckv_ref, kr_ref, qlat_ref, qr_ref = out_refs
        qnb = qn.astype(BF16)
        for h in range(N_HEADS):
            qlat_ref[:, h * KV_LORA:(h + 1) * KV_LORA] = _dot(
                qnb[:, h * D_NOPE:(h + 1) * D_NOPE], wa_ref[h * D_NOPE:(h + 1) * D_NOPE, :]).astype(BF16)
    else:
        ckv_ref, kr_ref, qn_ref, qr_ref, kn_ref, krp_ref, v_ref = out_refs
        qn_ref[...] = qn.astype(BF16)
        kn_ref[...] = _dot(ckvb, wa_ref[...]).astype(BF16)
        v_ref[...] = _dot(ckvb, wb_ref[...]).astype(BF16)
        krp_ref[...] = kr.astype(BF16)
    ckv_ref[...] = c_kv
    kr_ref[...] = kr[:, :D_ROPE]
    qr_ref[...] = qr.astype(BF16)


def _rope_cs(pos):
    inv = 1.0 / (ROPE_THETA ** (jnp.arange(0, D_ROPE, 2, dtype=F32) / D_ROPE))
    ang = pos.astype(F32)[:, None] * inv[None, :]
    cos, sin = jnp.cos(ang), jnp.sin(ang)
    z = jnp.zeros((pos.shape[0], LANES - D_ROPE), F32)
    return jnp.concatenate([cos, cos, z, -sin, sin, z], axis=1)


def _mla_weights(w_in, g_q, g_kv, w_uq, w_uk, w_uv, w_o):
    half = D_ROPE // 2
    w_cq = w_in[:, :Q_LORA].astype(BF16)
    w_ckv = w_in[:, Q_LORA:Q_LORA + KV_LORA].astype(BF16)
    w_k = w_in[:, Q_LORA + KV_LORA:]
    zk = jnp.zeros((D_MODEL, LANES - D_ROPE), F32)
    w_kr = jnp.concatenate([w_k, zk, w_k[:, half:], w_k[:, :half], zk], axis=1).astype(BF16)
    wq = w_uq.reshape(Q_LORA, N_HEADS, D_NOPE + D_ROPE)
    wq_n = wq[:, :, :D_NOPE].reshape(Q_LORA, N_HEADS * D_NOPE).astype(BF16)
    r = wq[:, :, D_NOPE:]
    zq = jnp.zeros((Q_LORA, N_HEADS, LANES - D_ROPE), F32)
    wq_r = jnp.concatenate([r, zq], axis=2).reshape(Q_LORA, N_HEADS * LANES).astype(BF16)
    wq_s = jnp.concatenate([r[:, :, half:], r[:, :, :half], zq], axis=2).reshape(Q_LORA, N_HEADS * LANES).astype(BF16)
    return dict(w_cq=w_cq, w_ckv=w_ckv, w_kr=w_kr, g_q=g_q.reshape(1, -1), g_kv=g_kv.reshape(1, -1),
                wq_n=wq_n, wq_r=wq_r, wq_s=wq_s, w_uk=w_uk.astype(BF16), w_ukT=w_uk.T.astype(BF16),
                w_uv=w_uv.astype(BF16), w_o=w_o.astype(BF16))


def _mla_proj(x, cs, w, absorb, tn):
    n = x.shape[0]
    hw = N_HEADS * LANES
    row = lambda width: pl.BlockSpec((tn, width), lambda i: (i, 0))
    wa = w["w_ukT"] if absorb else w["w_uk"]
    wb = w["w_uv"]
    in_specs = [row(D_MODEL), _const_spec(w["w_cq"].shape), _const_spec(w["w_ckv"].shape),
                _const_spec(w["w_kr"].shape), _const_spec(w["g_q"].shape), _const_spec(w["g_kv"].shape),
                _const_spec(w["wq_n"].shape), _const_spec(w["wq_r"].shape), _const_spec(w["wq_s"].shape),
                row(2 * LANES), _const_spec(wa.shape), _const_spec(wb.shape)]
    if absorb:
        out_shape = [jax.ShapeDtypeStruct((n, KV_LORA), F32), jax.ShapeDtypeStruct((n, D_ROPE), F32),
                     jax.ShapeDtypeStruct((n, N_HEADS * KV_LORA), BF16), jax.ShapeDtypeStruct((n, hw), BF16)]
        out_specs = [row(KV_LORA), row(D_ROPE), row(N_HEADS * KV_LORA), row(hw)]
    else:
        out_shape = [jax.ShapeDtypeStruct((n, KV_LORA), F32), jax.ShapeDtypeStruct((n, D_ROPE), F32),
                     jax.ShapeDtypeStruct((n, hw), BF16), jax.ShapeDtypeStruct((n, hw), BF16),
                     jax.ShapeDtypeStruct((n, hw), BF16), jax.ShapeDtypeStruct((n, LANES), BF16),
                     jax.ShapeDtypeStruct((n, hw), BF16)]
        out_specs = [row(KV_LORA), row(D_ROPE), row(hw), row(hw), row(hw), row(LANES), row(hw)]
    return pl.pallas_call(
        functools.partial(_mla_proj_kernel, absorb),
        grid=(n // tn,), in_specs=in_specs, out_specs=out_specs, out_shape=out_shape,
        compiler_params=_cparams(("arbitrary",)), name="mla_proj_absorb" if absorb else "mla_proj",
    )(x, w["w_cq"], w["w_ckv"], w["w_kr"], w["g_q"], w["g_kv"], w["wq_n"], w["wq_r"], w["wq_s"], cs, wa, wb)


def _flash_kernel(qn_ref, qr_ref, kn_ref, kr_ref, v_ref, x_ref, wo_ref, g_ref, b_ref, o_ref,
                  m_ref, acc_ref):
    qi = pl.program_id(0)
    ki = pl.program_id(1)
    tq = qn_ref.shape[0]
    tk = kn_ref.shape[0]

    @pl.when(ki == 0)
    def _():
        m_ref[...] = jnp.full(m_ref.shape, NEG_BIG, F32)
        acc_ref[...] = jnp.zeros(acc_ref.shape, F32)

    def step(masked):
        kr = kr_ref[...]
        ones = jnp.ones((tk, LANES), BF16)
        if masked:
            keep = (lax.broadcasted_iota(jnp.int32, (tq, tk), 1)
                    <= lax.broadcasted_iota(jnp.int32, (tq, tk), 0))

        def scores(h):
            hs = slice(h * LANES, (h + 1) * LANES)
            q = jnp.concatenate([qn_ref[:, hs], qr_ref[:, hs]], axis=1)
            k = jnp.concatenate([kn_ref[:, hs], kr], axis=1)
            return _dot_nt(q, k)

        s_next = scores(0)
        for h in range(N_HEADS):
            s = s_next
            if h + 1 < N_HEADS:
                s_next = scores(h + 1)
            if masked:
                s = jnp.where(keep, s, -jnp.inf)
            hs = slice(h * LANES, (h + 1) * LANES)
            m_prev = m_ref[h]
            m_new = jnp.maximum(m_prev, jnp.max(s, -1, keepdims=True))
            p = jnp.exp2(s - jnp.concatenate([m_new] * (tk // LANES), axis=1))
            corr = jnp.exp2(m_prev - m_new)
            v1 = jnp.concatenate([v_ref[:, hs], ones], axis=1)
            acc_ref[h] = jnp.concatenate([corr, corr], axis=1) * acc_ref[h] + _dot(p.astype(BF16), v1)
            m_ref[h] = m_new

    @pl.when(ki < qi)
    def _():
        step(False)

    @pl.when(ki == qi)
    def _():
        step(True)
        o = jnp.concatenate([acc_ref[h, :, :D_V] / acc_ref[h, :, D_V:] for h in range(N_HEADS)], axis=1)
        z = ALPHA * x_ref[...] + _dot(o.astype(BF16), wo_ref[...])
        o_ref[...] = _layer_norm(z, g_ref[...], b_ref[...])


def _flash_prompt(qn, qr, kn, krp, v, x, w_o, g, b, tq):
    n = x.shape[0]
    hw = N_HEADS * LANES
    qspec = lambda width: pl.BlockSpec((tq, width), lambda i, j: (i, 0))
    kspec = lambda width: pl.BlockSpec((tq, width), lambda i, j: (jnp.minimum(i, j), 0))
    return pl.pallas_call(
        _flash_kernel,
        grid=(n // tq, n // tq),
        in_specs=[qspec(hw), qspec(hw), kspec(hw), kspec(LANES), kspec(hw), qspec(D_MODEL),
                  _const_spec(w_o.shape), _const_spec(g.shape), _const_spec(b.shape)],
        out_specs=qspec(D_MODEL),
        out_shape=jax.ShapeDtypeStruct((n, D_MODEL), F32),
        scratch_shapes=[pltpu.VMEM((N_HEADS, tq, LANES), F32), pltpu.VMEM((N_HEADS, tq, D_V + LANES), F32)],
        compiler_params=_cparams(("arbitrary", "arbitrary")), name="mla_flash_prompt",
    )(qn, qr, kn, krp, v, x, w_o, g, b)


PAGES_PER_STEP = 32
ROWS = 32


def _decode_kernel(pt_ref, ql_ref, qr_ref, cn_ref, rn_ref, *rest):
    lat_refs = rest[:PAGES_PER_STEP]
    rope_refs = rest[PAGES_PER_STEP:2 * PAGES_PER_STEP]
    o_ref, m_ref, l_ref, acc_ref, kcat_ref, rcat_ref = rest[2 * PAGES_PER_STEP:]
    c = pl.program_id(1)
    nc = pl.num_programs(1)

    @pl.when(c == 0)
    def _():
        m_ref[...] = jnp.full(m_ref.shape, NEG_BIG, F32)
        l_ref[...] = jnp.zeros(l_ref.shape, F32)
        acc_ref[...] = jnp.zeros(acc_ref.shape, F32)

    ql = ql_ref[...]
    qr = qr_ref[:, :D_ROPE]
    for p in range(PAGES_PER_STEP):
        kcat_ref[p * PAGE_SIZE:(p + 1) * PAGE_SIZE, :] = lat_refs[p][...].astype(BF16)
        rcat_ref[:, p * PAGE_SIZE:(p + 1) * PAGE_SIZE] = rope_refs[p][...].astype(BF16)
    kcat = kcat_ref[...]
    s = _dot_nt(ql, kcat) + _dot(qr, rcat_ref[...])
    m_prev = m_ref[...]
    m_new = jnp.maximum(m_prev, jnp.max(s, -1, keepdims=True))
    corr = jnp.exp(m_prev - m_new)
    pp = jnp.exp(s - m_new)
    m_ref[...] = m_new
    l_ref[...] = corr * l_ref[...] + jnp.sum(pp, -1, keepdims=True)
    acc_ref[...] = corr * acc_ref[...] + _dot(pp.astype(BF16), kcat)

    @pl.when(c == nc - 1)
    def _():
        qlf = ql.astype(F32)
        qrf = qr.astype(F32)
        cn = cn_ref[...].astype(BF16).astype(F32)
        rn = rn_ref[...].astype(BF16).astype(F32)
        t_of_row = lax.broadcasted_iota(jnp.int32, (ROWS, 1), 0) // N_HEADS
        n_new = cn.shape[0]
        s_new = []
        for j in range(n_new):
            sj = (jnp.sum(qlf * cn[j:j + 1, :], -1, keepdims=True)
                  + jnp.sum(qrf * rn[j:j + 1, :], -1, keepdims=True))
            s_new.append(jnp.where(j <= t_of_row, sj, -jnp.inf))
        m0 = m_ref[...]
        m1 = m0
        for sj in s_new:
            m1 = jnp.maximum(m1, sj)
        corr1 = jnp.exp(m0 - m1)
        l1 = corr1 * l_ref[...]
        a1 = corr1 * acc_ref[...]
        for j in range(n_new):
            pj = jnp.exp(s_new[j] - m1)
            l1 = l1 + pj
            a1 = a1 + pj.astype(BF16).astype(F32) * cn[j:j + 1, :]
        o_ref[...] = a1 / l1


def _decode(qlat, qr, ckv_new, kr_new, cache_lat, cache_kr, page_table):
    nb, n_pages = page_table.shape
    nc = n_pages // PAGES_PER_STEP
    n_new = ckv_new.shape[1]

    def page_spec(p, rows, width):
        return pl.BlockSpec((None, None, rows, width),
                            lambda b, c, pt: (0, pt[b, c * PAGES_PER_STEP + p], 0, 0))

    per_b = lambda rows, width: pl.BlockSpec((None, rows, width), lambda b, c, pt: (b, 0, 0))
    in_specs = ([per_b(ROWS, KV_LORA), per_b(ROWS, LANES), per_b(n_new, KV_LORA), per_b(n_new, D_ROPE)]
                + [page_spec(p, PAGE_SIZE, KV_LORA) for p in range(PAGES_PER_STEP)]
                + [page_spec(p, D_ROPE, PAGE_SIZE) for p in range(PAGES_PER_STEP)])
    keys = PAGES_PER_STEP * PAGE_SIZE
    grid_spec = pltpu.PrefetchScalarGridSpec(
        num_scalar_prefetch=1, grid=(nb, nc), in_specs=in_specs,
        out_specs=per_b(ROWS, KV_LORA),
        scratch_shapes=[pltpu.VMEM((ROWS, 1), F32), pltpu.VMEM((ROWS, 1), F32), pltpu.VMEM((ROWS, KV_LORA), F32),
                        pltpu.VMEM((keys, KV_LORA), BF16), pltpu.VMEM((D_ROPE, keys), BF16)])
    return pl.pallas_call(
        _decode_kernel, grid_spec=grid_spec,
        out_shape=jax.ShapeDtypeStruct((nb, ROWS, KV_LORA), F32),
        compiler_params=_cparams(("arbitrary", "arbitrary")), name="mla_decode",
    )(page_table, qlat, qr, ckv_new, kr_new, *([cache_lat] * PAGES_PER_STEP), *([cache_kr] * PAGES_PER_STEP))


def _sample_out_kernel(ol_ref, x_ref, wuv_ref, wo_ref, g_ref, b_ref, o_ref):
    olb = ol_ref[...].astype(BF16)
    o = jnp.concatenate(
        [_dot(olb[:, h * KV_LORA:(h + 1) * KV_LORA], wuv_ref[:, h * D_V:(h + 1) * D_V]) for h in range(N_HEADS)],
        axis=1)
    z = ALPHA * x_ref[...] + _dot(o.astype(BF16), wo_ref[...])
    o_ref[...] = _layer_norm(z, g_ref[...], b_ref[...])


def _sample_out(o_lat, x, w_uv, w_o, g, b):
    n = x.shape[0]
    args = (o_lat, x, w_uv, w_o, g, b)
    return pl.pallas_call(
        _sample_out_kernel, grid=(1,), in_specs=[_const_spec(a.shape) for a in args],
        out_specs=_const_spec((n, D_MODEL)), out_shape=jax.ShapeDtypeStruct((n, D_MODEL), F32),
        compiler_params=_cparams(("arbitrary",)), name="mla_sample_out",
    )(*args)


def _sort16_network():
    def merge(lo, hi, r):
        step = r * 2
        if step < hi - lo:
            yield from merge(lo, hi, step)
            yield from merge(lo + r, hi, step)
            yield from [(i, i + r) for i in range(lo + r, hi - r, step)]
        else:
            yield (lo, lo + r)

    def sort(lo, hi):
        if hi - lo >= 1:
            mid = lo + (hi - lo) // 2
            yield from sort(lo, mid)
            yield from sort(mid + 1, hi)
            yield from merge(lo, hi, 1)

    return tuple(sort(0, PEER_TOPK - 1))


_SORT16 = _sort16_network()


def _sort16_desc(v):
    v = list(v)
    for i, j in _SORT16:
        v[i], v[j] = jnp.maximum(v[i], v[j]), jnp.minimum(v[i], v[j])
    return v


def _bitonic_merge16_desc(v):
    v = list(v)
    d = PEER_TOPK // 2
    while d >= 1:
        for i in range(PEER_TOPK):
            if i & d == 0:
                v[i], v[i + d] = jnp.maximum(v[i], v[i + d]), jnp.minimum(v[i], v[i + d])
        d //= 2
    return v


def _merge_top16(x, y):
    return _bitonic_merge16_desc([jnp.maximum(x[k], y[PEER_TOPK - 1 - k]) for k in range(PEER_TOPK)])


def _top16_rows(s):
    v = _sort16_desc([s[SUBLANES * k:SUBLANES * (k + 1), :] for k in range(N_KEYS // SUBLANES)])
    for shift in (4, 2, 1):
        v = _merge_top16(v, [pltpu.roll(a, shift, 0) for a in v])
    return v


def _pair_region():
    return [(a, b) for a in range(PEER_TOPK) for b in range(PEER_TOPK) if (a + 1) * (b + 1) <= PEER_TOPK]


def _peer_route_kernel(x_ref, wqt_ref, keys_ref, s1_ref, e1_ref, s2_ref, e2_ref, tau_ref):
    tn = x_ref.shape[0]
    xb = x_ref[...].astype(BF16)
    qt = _dot_nt(wqt_ref[...], xb)
    sub = lax.broadcasted_iota(jnp.int32, (SUBLANES, tn), 0)
    tops = []
    packed = [[jnp.zeros((SUBLANES, tn), F32) for _ in range(PEER_TOPK)] for _ in range(2)]
    scores = []
    for h in range(PEER_HEADS):
        for half in range(2):
            c = 2 * h + half
            st = _dot(keys_ref[c], qt[c * HALF_KEY:(c + 1) * HALF_KEY, :].astype(BF16))
            scores.append(st)
            top = _top16_rows(st)
            tops.append(top[0])
            for k in range(PEER_TOPK):
                packed[half][k] = jnp.where(sub == h, top[k], packed[half][k])
    cands = [packed[0][a] + packed[1][b] for a, b in _pair_region()]
    pad = (-len(cands)) % PEER_TOPK
    cands += [jnp.full((SUBLANES, tn), -jnp.inf, F32)] * pad
    best = _sort16_desc(cands[:PEER_TOPK])
    for g in range(1, len(cands) // PEER_TOPK):
        best = _merge_top16(best, _sort16_desc(cands[g * PEER_TOPK:(g + 1) * PEER_TOPK]))
    tau = best[PEER_TOPK - 1]
    z = jnp.zeros((SUBLANES, tn), F32)
    for k in range(PEER_TOPK):
        z = z + jnp.exp(best[k] - best[0])
    inv_z = 1.0 / z
    tau_ref[...] = tau
    for h in range(PEER_HEADS):
        m1 = jnp.concatenate([tops[2 * h]] * (N_KEYS // SUBLANES), axis=0)
        m2 = jnp.concatenate([tops[2 * h + 1]] * (N_KEYS // SUBLANES), axis=0)
        s1_ref[h] = scores[2 * h]
        s2_ref[h] = scores[2 * h + 1]
        e1_ref[h] = jnp.exp(scores[2 * h] - m1) * inv_z[h:h + 1, :]
        e2_ref[h] = jnp.exp(scores[2 * h + 1] - m2)


def _peer_route(x, w_qt, keys, tn):
    n = x.shape[0]
    big = jax.ShapeDtypeStruct((PEER_HEADS, N_KEYS, n), F32)
    bspec = pl.BlockSpec((PEER_HEADS, N_KEYS, tn), lambda i: (0, 0, i))
    return pl.pallas_call(
        _peer_route_kernel, grid=(n // tn,),
        in_specs=[pl.BlockSpec((tn, D_MODEL), lambda i: (i, 0)), _const_spec(w_qt.shape), _const_spec(keys.shape)],
        out_specs=[bspec, bspec, bspec, bspec, pl.BlockSpec((PEER_HEADS, tn), lambda i: (0, i))],
        out_shape=[big, big, big, big, jax.ShapeDtypeStruct((PEER_HEADS, n), F32)],
        compiler_params=_cparams(("arbitrary",)), name="peer_route",
    )(x, w_qt, keys)


PEER_TE = SUBLANES * N_KEYS
PEER_JB = 2


def _peer_dense_kernel(x_ref, u_ref, vt_ref, s1_ref, e1_ref, s2_ref, e2_ref, tau_ref, g_ref, b_ref,
                       o_ref, acc_ref, h_ref, gate_ref, xb_ref, s1b_ref, e1b_ref, taub_ref):
    j = pl.program_id(1)
    last = pl.num_programs(1) - 1
    tm = x_ref.shape[0]

    def weighted_prev():
        return (_gelu_tanh(h_ref[...]) * gate_ref[...]).astype(BF16)

    @pl.when(j == 0)
    def _():
        acc_ref[...] = jnp.zeros(acc_ref.shape, F32)
        h_ref[...] = jnp.zeros(h_ref.shape, F32)
        gate_ref[...] = jnp.zeros(gate_ref.shape, F32)
        xb_ref[...] = x_ref[...].astype(BF16)
        for h in range(PEER_HEADS):
            taub_ref[h * SUBLANES:(h + 1) * SUBLANES, :] = jnp.broadcast_to(tau_ref[h:h + 1, :], (SUBLANES, tm))

    @pl.when(j < last)
    def _():
        acc_ref[...] += _dot(vt_ref[...], weighted_prev())
        h_ref[...] = _dot_nt(u_ref[...], xb_ref[...])
        for h in range(PEER_HEADS):
            for ii in range(SUBLANES):
                brow = slice((h * SUBLANES + ii) * SUBLANES, (h * SUBLANES + ii + 1) * SUBLANES)
                s1b_ref[brow, :] = jnp.broadcast_to(s1_ref[h, ii:ii + 1, :], (SUBLANES, tm))
                e1b_ref[brow, :] = jnp.broadcast_to(e1_ref[h, ii:ii + 1, :], (SUBLANES, tm))

        def chunk(k, carry):
            j0 = k * (PEER_JB * SUBLANES)
            for lc in range(tm // LANES):
                ls = slice(lc * LANES, (lc + 1) * LANES)
                gates = [[jnp.zeros((SUBLANES, LANES), F32) for _ in range(SUBLANES)] for _ in range(PEER_JB)]
                for h in range(PEER_HEADS):
                    tau = taub_ref[h * SUBLANES:(h + 1) * SUBLANES, ls]
                    jrows = [pl.ds(pl.multiple_of(j0 + t * SUBLANES, SUBLANES), SUBLANES) for t in range(PEER_JB)]
                    s2 = [s2_ref[h, jrows[t], ls] for t in range(PEER_JB)]
                    e2 = [e2_ref[h, jrows[t], ls] for t in range(PEER_JB)]
                    for ii in range(SUBLANES):
                        brow = slice((h * SUBLANES + ii) * SUBLANES, (h * SUBLANES + ii + 1) * SUBLANES)
                        s1 = s1b_ref[brow, ls]
                        e1 = e1b_ref[brow, ls]
                        for t in range(PEER_JB):
                            gates[t][ii] = gates[t][ii] + jnp.where(s2[t] + s1 >= tau, e2[t] * e1, 0.0)
                for t in range(PEER_JB):
                    for ii in range(SUBLANES):
                        r0 = pl.multiple_of(ii * N_KEYS + j0 + t * SUBLANES, SUBLANES)
                        gate_ref[pl.ds(r0, SUBLANES), ls] = gates[t][ii]
            return carry

        lax.fori_loop(0, N_KEYS // (PEER_JB * SUBLANES), chunk, 0)

    @pl.when(j == last)
    def _():
        acc = acc_ref[...] + _dot(vt_ref[...], weighted_prev())
        z = ALPHA * x_ref[...] + acc.T
        o_ref[...] = _layer_norm(z, g_ref[...], b_ref[...])


def _peer_dense(x, u, vt, s1, e1, s2, e2, tau, g, b, tm):
    n = x.shape[0]
    nt = N_EXPERTS // PEER_TE
    full = pl.BlockSpec((PEER_HEADS, N_KEYS, tm), lambda i, j: (0, 0, i))
    part = pl.BlockSpec((PEER_HEADS, SUBLANES, tm), lambda i, j: (0, jnp.minimum(j, nt - 1), i))
    nb = PEER_HEADS * SUBLANES * SUBLANES
    return pl.pallas_call(
        _peer_dense_kernel, grid=(n // tm, nt + 1),
        in_specs=[pl.BlockSpec((tm, D_MODEL), lambda i, j: (i, 0)),
                  pl.BlockSpec((PEER_TE, D_MODEL), lambda i, j: (jnp.minimum(j, nt - 1), 0)),
                  pl.BlockSpec((D_MODEL, PEER_TE), lambda i, j: (0, jnp.maximum(j - 1, 0))),
                  part, part, full, full,
                  pl.BlockSpec((PEER_HEADS, tm), lambda i, j: (0, i)),
                  _const_spec(g.shape), _const_spec(b.shape)],
        out_specs=pl.BlockSpec((tm, D_MODEL), lambda i, j: (i, 0)),
        out_shape=jax.ShapeDtypeStruct((n, D_MODEL), F32),
        scratch_shapes=[pltpu.VMEM((D_MODEL, tm), F32), pltpu.VMEM((PEER_TE, tm), F32),
                        pltpu.VMEM((PEER_TE, tm), F32), pltpu.VMEM((tm, D_MODEL), BF16),
                        pltpu.VMEM((nb, tm), F32), pltpu.VMEM((nb, tm), F32),
                        pltpu.VMEM((PEER_HEADS * SUBLANES, tm), F32)],
        compiler_params=_cparams(("arbitrary", "arbitrary")), name="peer_dense",
    )(x, u, vt, s1, e1, s2, e2, tau, g, b)


def _peer_weights(w_q, sub_keys, exp_u, exp_v):
    return dict(w_qt=w_q.T.astype(BF16),
                keys=sub_keys.reshape(2 * PEER_HEADS, N_KEYS, HALF_KEY).astype(BF16),
                u=exp_u.astype(BF16), vt=exp_v.T.astype(BF16))


def _peer_ln(x, w, g, b, tn_route, tm):
    s1, e1, s2, e2, tau = _peer_route(x, w["w_qt"], w["keys"], tn_route)
    return _peer_dense(x, w["u"], w["vt"], s1, e1, s2, e2, tau, g, b, tm)


def _s5_disc_kernel(are_ref, aim_ref, ldt_ref, bre_ref, bim_ref, abre_ref, abim_ref, bbre_ref, bbim_ref):
    lr = jnp.minimum(are_ref[...], -1e-4)
    li = aim_ref[...]
    dt = jnp.exp(ldt_ref[...])
    mag = jnp.exp(lr * dt)
    ab_re = mag * jnp.cos(li * dt)
    ab_im = mag * jnp.sin(li * dt)
    den = lr * lr + li * li
    nr, ni = ab_re - 1.0, ab_im
    f_re = (nr * lr + ni * li) / den
    f_im = (ni * lr - nr * li) / den
    br, bi = bre_ref[...], bim_ref[...]
    abre_ref[...] = ab_re
    abim_ref[...] = ab_im
    bbre_ref[...] = f_re * br - f_im * bi
    bbim_ref[...] = f_re * bi + f_im * br


def _s5_weights(w_in, a_re, a_im, log_dt, b_re, b_im, c_re, c_im, d_skip, w_glu, w_o):
    rep = lambda a: jnp.repeat(a, GROUP_SIZE, axis=0)
    bt = lambda b: b.transpose(0, 2, 1).reshape(D_MODEL, STATE)
    args = (rep(a_re), rep(a_im), rep(log_dt.reshape(N_GROUPS, 1)), bt(b_re), bt(b_im))
    sd = jax.ShapeDtypeStruct((D_MODEL, STATE), F32)
    ab_re, ab_im, bb_re, bb_im = pl.pallas_call(
        _s5_disc_kernel, grid=(1,), in_specs=[_const_spec(a.shape) for a in args],
        out_specs=[_const_spec(sd.shape)] * 4, out_shape=[sd] * 4,
        compiler_params=_cparams(("arbitrary",)), name="s5_discretize")(*args)
    nblk = 4
    gl = N_GROUPS // nblk
    eye = jnp.eye(gl, dtype=F32)

    def bdiag_in(bb):
        t = bb.reshape(nblk, gl, GROUP_SIZE, STATE)
        return jnp.einsum('kgpn,gh->kgphn', t, eye).reshape(nblk, gl * GROUP_SIZE, gl * STATE).astype(BF16)

    def bdiag_out(c):
        t = c.reshape(nblk, gl, GROUP_SIZE, STATE)
        return jnp.einsum('kgpn,gh->kgnhp', t, eye).reshape(nblk, gl * STATE, gl * GROUP_SIZE)

    c_cat = jnp.concatenate([bdiag_out(c_re), -bdiag_out(c_im)], axis=1).astype(BF16)
    return dict(w_in=w_in.astype(BF16), b_re=bdiag_in(bb_re), b_im=bdiag_in(bb_im),
                a_re=ab_re[::GROUP_SIZE].reshape(1, N_STATE), a_im=ab_im[::GROUP_SIZE].reshape(1, N_STATE),
                c_cat=c_cat, d=d_skip.reshape(1, D_MODEL), w_glu=w_glu.astype(BF16), w_o=w_o.astype(BF16))


S5_LW = 2 * LANES
S5_NBLK = 4


def _cmul(ar, ai, br, bi):
    return ar * br - ai * bi, ar * bi + ai * br


def _s5_scan_kernel(seg, x_ref, win_ref, bre_ref, bim_ref, are_ref, aim_ref, ccat_ref, d_ref, *rest):
    if seg:
        s0re_ref, s0im_ref, g_ref, ore_ref, oim_ref, sre_ref, sim_ref, tab_ref = rest
    else:
        g_ref, ore_ref, oim_ref, sre_ref, sim_ref, tab_ref, car_ref = rest
    i = pl.program_id(0)
    tl = x_ref.shape[0]
    period = seg if seg else SUBLANES

    @pl.when(i == 0)
    def _():
        ar = jnp.broadcast_to(are_ref[...], (SUBLANES, N_STATE))
        ai = jnp.broadcast_to(aim_ref[...], (SUBLANES, N_STATE))
        rr = lax.broadcasted_iota(jnp.int32, (SUBLANES, N_STATE), 0) % period
        pr, pi = ar, ai
        powers = [(pr, pi)]
        for _ in range(SUBLANES - 1):
            pr, pi = _cmul(pr, pi, ar, ai)
            powers.append((pr, pi))
        for k, d in enumerate((1, 2, 4)):
            tab_ref[2 * k] = jnp.where(rr >= d, powers[d - 1][0], 0.0)
            tab_ref[2 * k + 1] = jnp.where(rr >= d, powers[d - 1][1], 0.0)
        cr = jnp.zeros((SUBLANES, N_STATE), F32)
        ci = jnp.zeros((SUBLANES, N_STATE), F32)
        for p in range(SUBLANES):
            cr = jnp.where(rr == p, powers[p][0], cr)
            ci = jnp.where(rr == p, powers[p][1], ci)
        tab_ref[6] = cr
        tab_ref[7] = ci
        if not seg:
            car_ref[...] = jnp.zeros(car_ref.shape, F32)

    u = _dot(x_ref[...].astype(BF16), win_ref[...])
    ub = u.astype(BF16)
    kin = D_MODEL // S5_NBLK
    kst = N_STATE // S5_NBLK
    for k in range(S5_NBLK):
        sre_ref[:, k * kst:(k + 1) * kst] = _dot(ub[:, k * kin:(k + 1) * kin], bre_ref[k])
        sim_ref[:, k * kst:(k + 1) * kst] = _dot(ub[:, k * kin:(k + 1) * kin], bim_ref[k])

    steps = tuple(d for d in (1, 2, 4) if d < period)

    def body(gi, carry):
        rows = pl.ds(pl.multiple_of(gi * SUBLANES, SUBLANES), SUBLANES)
        for lc in range(N_STATE // S5_LW):
            ls = slice(lc * S5_LW, (lc + 1) * S5_LW)
            xr = sre_ref[rows, ls]
            xi = sim_ref[rows, ls]
            for d in steps:
                k = (1, 2, 4).index(d)
                yr, yi = _cmul(tab_ref[2 * k, :, ls], tab_ref[2 * k + 1, :, ls],
                               pltpu.roll(xr, d, 0), pltpu.roll(xi, d, 0))
                xr, xi = xr + yr, xi + yi
            if seg:
                cr, ci = s0re_ref[rows, ls], s0im_ref[rows, ls]
            else:
                cr, ci = car_ref[0, :, ls], car_ref[1, :, ls]
            yr, yi = _cmul(tab_ref[6, :, ls], tab_ref[7, :, ls], cr, ci)
            xr, xi = xr + yr, xi + yi
            sre_ref[rows, ls] = xr
            sim_ref[rows, ls] = xi
            if not seg:
                car_ref[0, :, ls] = jnp.broadcast_to(xr[SUBLANES - 1:SUBLANES, :], (SUBLANES, S5_LW))
                car_ref[1, :, ls] = jnp.broadcast_to(xi[SUBLANES - 1:SUBLANES, :], (SUBLANES, S5_LW))
        return carry

    lax.fori_loop(0, tl // SUBLANES, body, 0)

    if seg:
        ore_ref[...] = sre_ref[...]
        oim_ref[...] = sim_ref[...]
    else:
        ore_ref[...] = car_ref[0, 0:1, :]
        oim_ref[...] = car_ref[1, 0:1, :]

    kout = D_MODEL // S5_NBLK
    ys = []
    for k in range(S5_NBLK):
        st = jnp.concatenate([sre_ref[:, k * kst:(k + 1) * kst], sim_ref[:, k * kst:(k + 1) * kst]], axis=1)
        ys.append(_dot(st.astype(BF16), ccat_ref[k]))
    y = jnp.concatenate(ys, axis=1) + d_ref[...] * u
    g_ref[...] = _gelu_tanh(y).astype(BF16)


def _s5_scan(x, w, tl, s0=None):
    n = x.shape[0]
    seg = 0 if s0 is None else 4
    row = lambda width: pl.BlockSpec((tl, width), lambda i: (i, 0))
    wargs = (w["w_in"], w["b_re"], w["b_im"], w["a_re"], w["a_im"], w["c_cat"], w["d"])
    in_specs = [row(D_MODEL)] + [_const_spec(a.shape) for a in wargs]
    args = (x,) + wargs
    scratch = [pltpu.VMEM((tl, N_STATE), F32), pltpu.VMEM((tl, N_STATE), F32),
               pltpu.VMEM((8, SUBLANES, N_STATE), F32)]
    if seg:
        in_specs += [row(N_STATE), row(N_STATE)]
        args += tuple(s0)
        st_shape = jax.ShapeDtypeStruct((n, N_STATE), F32)
        st_spec = row(N_STATE)
    else:
        scratch.append(pltpu.VMEM((2, SUBLANES, N_STATE), F32))
        st_shape = jax.ShapeDtypeStruct((1, N_STATE), F32)
        st_spec = _const_spec((1, N_STATE))
    return pl.pallas_call(
        functools.partial(_s5_scan_kernel, seg), grid=(n // tl,), in_specs=in_specs,
        out_specs=[row(D_MODEL), st_spec, st_spec],
        out_shape=[jax.ShapeDtypeStruct((n, D_MODEL), BF16), st_shape, st_shape],
        scratch_shapes=scratch,
        compiler_params=_cparams(("arbitrary",)), name="s5_scan_seg" if seg else "s5_scan",
    )(*args)


def _glu_out_kernel(gin_ref, x_ref, wglu_ref, wo_ref, g_ref, b_ref, o_ref):
    ga = _dot(gin_ref[...], wglu_ref[...])
    hid = ga[:, :D_MODEL] * jax.nn.sigmoid(ga[:, D_MODEL:])
    z = ALPHA * x_ref[...] + _dot(hid.astype(BF16), wo_ref[...])
    o_ref[...] = _layer_norm(z, g_ref[...], b_ref[...])


def _glu_out(gin, x, w_glu, w_o, g, b, tn):
    n = x.shape[0]
    row = lambda: pl.BlockSpec((tn, D_MODEL), lambda i: (i, 0))
    return pl.pallas_call(
        _glu_out_kernel, grid=(n // tn,),
        in_specs=[row(), row(), _const_spec(w_glu.shape), _const_spec(w_o.shape),
                  _const_spec(g.shape), _const_spec(b.shape)],
        out_specs=row(), out_shape=jax.ShapeDtypeStruct((n, D_MODEL), F32),
        compiler_params=_cparams(("arbitrary",)), name="s5_glu_out",
    )(gin, x, w_glu, w_o, g, b)


def _tile(n, pref):
    return pref if n % pref == 0 else n


def kernel(x_prompt, x_sample, cache_kv_latent, cache_k_rope, state_ssm_re, state_ssm_im, page_table, mla_w_in, mla_g_q, mla_g_kv, mla_w_uq, mla_w_uk, mla_w_uv, mla_w_o, ssm_w_in, ssm_a_re, ssm_a_im, ssm_log_dt, ssm_b_re, ssm_b_im, ssm_c_re, ssm_c_im, ssm_d, ssm_w_glu, ssm_w_o, peer_w_q, peer_sub_keys, peer_u, peer_v, ln_g, ln_b):
    bp, lp, _ = x_prompt.shape
    bd, td, _ = x_sample.shape
    assert bp == 1 and td * N_HEADS == ROWS
    npr, nsm = bp * lp, bd * td
    yp = x_prompt.reshape(npr, D_MODEL)
    ys = x_sample.reshape(nsm, D_MODEL)
    lng = lambda layer, k: ln_g[layer, k].reshape(1, D_MODEL)
    lnb = lambda layer, k: ln_b[layer, k].reshape(1, D_MODEL)

    outs = {}
    for layer in range(DEPTH):
        j = layer // 2
        if layer % 2 == 0:
            w = _mla_weights(mla_w_in[j], mla_g_q[j], mla_g_kv[j], mla_w_uq[j], mla_w_uk[j], mla_w_uv[j], mla_w_o[j])
            cs_p = _rope_cs(jnp.arange(lp))
            ckv_p, kr_p, qn, qr, kn, krp, v = _mla_proj(yp, cs_p, w, False, _tile(npr, 512))
            yp = _flash_prompt(qn, qr, kn, krp, v, yp, w["w_o"], lng(layer, 0), lnb(layer, 0), _tile(npr, 512))
            cs_s = _rope_cs(jnp.tile(PAST_LEN + jnp.arange(td), bd))
            ckv_s, kr_s, qlat, qr_s = _mla_proj(ys, cs_s, w, True, nsm)
            o_lat = _decode(qlat.reshape(bd, ROWS, KV_LORA), qr_s.reshape(bd, ROWS, LANES),
                            ckv_s.reshape(bd, td, KV_LORA), kr_s.reshape(bd, td, D_ROPE),
                            cache_kv_latent[j:j + 1], jnp.swapaxes(cache_k_rope[j:j + 1], 2, 3), page_table)
            ys = _sample_out(o_lat.reshape(nsm, N_HEADS * KV_LORA), ys, w["w_uv"], w["w_o"],
                             lng(layer, 0), lnb(layer, 0))
            outs.setdefault("p_lat", []).append(ckv_p.reshape(bp, lp, KV_LORA))
            outs.setdefault("p_kr", []).append(kr_p.reshape(bp, lp, D_ROPE))
            outs.setdefault("s_lat", []).append(ckv_s.reshape(bd, td, KV_LORA))
            outs.setdefault("s_kr", []).append(kr_s.reshape(bd, td, D_ROPE))
        else:
            w = _s5_weights(ssm_w_in[j], ssm_a_re[j], ssm_a_im[j], ssm_log_dt[j], ssm_b_re[j], ssm_b_im[j],
                            ssm_c_re[j], ssm_c_im[j], ssm_d[j], ssm_w_glu[j], ssm_w_o[j])
            g_p, sre_p, sim_p = _s5_scan(yp, w, _tile(npr, 256))
            yp = _glu_out(g_p, yp, w["w_glu"], w["w_o"], lng(layer, 0), lnb(layer, 0), _tile(npr, 512))
            s0 = (jnp.repeat(state_ssm_re[j].reshape(bd, N_STATE), td, axis=0),
                  jnp.repeat(state_ssm_im[j].reshape(bd, N_STATE), td, axis=0))
            g_s, sre_s, sim_s = _s5_scan(ys, w, _tile(nsm, 128), s0)
            ys = _glu_out(g_s, ys, w["w_glu"], w["w_o"], lng(layer, 0), lnb(layer, 0), _tile(nsm, 512))
            outs.setdefault("p_sre", []).append(sre_p.reshape(bp, N_GROUPS, STATE))
            outs.setdefault("p_sim", []).append(sim_p.reshape(bp, N_GROUPS, STATE))
            outs.setdefault("s_sre", []).append(sre_s[td - 1::td].reshape(bd, N_GROUPS, STATE))
            outs.setdefault("s_sim", []).append(sim_s[td - 1::td].reshape(bd, N_GROUPS, STATE))
        pw = _peer_weights(peer_w_q[layer], peer_sub_keys[layer], peer_u[layer], peer_v[layer])
        yp = _peer_ln(yp, pw, lng(layer, 1), lnb(layer, 1), _tile(npr, 256), _tile(npr, 512))
        ys = _peer_ln(ys, pw, lng(layer, 1), lnb(layer, 1), _tile(nsm, 256), _tile(nsm, 512))

    return (yp.reshape(bp, lp, D_MODEL), ys.reshape(bd, td, D_MODEL),
            jnp.stack(outs["p_lat"]), jnp.stack(outs["p_kr"]), jnp.stack(outs["p_sre"]), jnp.stack(outs["p_sim"]),
            jnp.stack(outs["s_lat"]), jnp.stack(outs["s_kr"]), jnp.stack(outs["s_sre"]), jnp.stack(outs["s_sim"]))
```

```python
import functools
import math

import jax
import jax.numpy as jnp
from jax import lax
from jax.experimental import pallas as pl
from jax.experimental.pallas import tpu as pltpu

F32 = jnp.float32
BF16 = jnp.bfloat16

D_MODEL = 1024
DEPTH = 2
PAST_LEN = 16384
PAGE_SIZE = 128

N_HEADS = 8
D_NOPE = 128
D_ROPE = 64
D_V = 128
Q_LORA = 384
KV_LORA = 256
ROPE_THETA = 10000.0
ATTN_SCALE = 1.0 / math.sqrt(D_NOPE + D_ROPE)

GROUP_SIZE = 16
N_GROUPS = D_MODEL // GROUP_SIZE
STATE = 64
N_STATE = N_GROUPS * STATE

PEER_HEADS = 8
N_KEYS = 128
N_EXPERTS = N_KEYS * N_KEYS
HALF_KEY = 128
PEER_TOPK = 16

ALPHA = (2 * DEPTH) ** 0.25
LN_EPS = 1e-5
RMS_EPS = 1e-6

LANES = 128
SUBLANES = 8
VMEM_LIMIT = 56 * 1024 * 1024

NEG_BIG = -1e30


def _cparams(sem):
    return pltpu.CompilerParams(dimension_semantics=sem, vmem_limit_bytes=VMEM_LIMIT)


def _dot(a, b):
    return jnp.dot(a, b, preferred_element_type=F32)


def _dot_nt(a, b):
    return lax.dot_general(a, b, (((1,), (1,)), ((), ())), preferred_element_type=F32)


def _layer_norm(z, g, b):
    mu = jnp.mean(z, -1, keepdims=True)
    zc = z - mu
    var = jnp.mean(zc * zc, -1, keepdims=True)
    return zc * lax.rsqrt(var + LN_EPS) * g + b


def _rms_norm(x, g):
    return x * lax.rsqrt(jnp.mean(x * x, -1, keepdims=True) + RMS_EPS) * g


def _gelu_tanh(x):
    c = math.sqrt(2.0 / math.pi)
    return x * (0.5 * (1.0 + jnp.tanh(c * (x + 0.044715 * (x * x * x)))))


def _const_spec(shape):
    n = len(shape)
    return pl.BlockSpec(shape, lambda *_: (0,) * n)


def _mla_proj_kernel(absorb, x_ref, wcq_ref, wckv_ref, wkr_ref, gq_ref, gkv_ref,
                     wuqn_ref, wuqr_ref, wuqs_ref, cs_ref, wa_ref, wb_ref, *out_refs):
    xb = x_ref[...].astype(BF16)
    c_q = _rms_norm(_dot(xb, wcq_ref[...]), gq_ref[...])
    c_kv = _rms_norm(_dot(xb, wckv_ref[...]), gkv_ref[...])
    hk = _dot(xb, wkr_ref[...])
    cos = cs_ref[:, :LANES]
    sin = cs_ref[:, LANES:]
    kr = hk[:, :LANES] * cos + hk[:, LANES:] * sin
    cqb = c_q.astype(BF16)
    cos8 = jnp.concatenate([cos] * N_HEADS, axis=1)
    sin8 = jnp.concatenate([sin] * N_HEADS, axis=1)
    qscale = ATTN_SCALE if absorb else ATTN_SCALE * math.log2(math.e)
    qn = _dot(cqb, wuqn_ref[...]) * qscale
    qr = (_dot(cqb, wuqr_ref[...]) * cos8 + _dot(cqb, wuqs_ref[...]) * sin8) * qscale
    ckvb = c_kv.astype(BF16)
    if absorb:
        ckv_ref, kr_ref, qlat_ref, qr_ref = out_refs
        qnb = qn.astype(BF16)
        for h in range(N_HEADS):
            qlat_ref[:, h * KV_LORA:(h + 1) * KV_LORA] = _dot(
                qnb[:, h * D_NOPE:(h + 1) * D_NOPE], wa_ref[h * D_NOPE:(h + 1) * D_NOPE, :]).astype(BF16)
    else:
        ckv_ref, kr_ref, qn_ref, qr_ref, kn_ref, krp_ref, v_ref = out_refs
        qn_ref[...] = qn.astype(BF16)
        kn_ref[...] = _dot(ckvb, wa_ref[...]).astype(BF16)
        v_ref[...] = _dot(ckvb, wb_ref[...]).astype(BF16)
        krp_ref[...] = kr.astype(BF16)
    ckv_ref[...] = c_kv
    kr_ref[...] = kr[:, :D_ROPE]
    qr_ref[...] = qr.astype(BF16)


def _rope_cs(pos):
    inv = 1.0 / (ROPE_THETA ** (jnp.arange(0, D_ROPE, 2, dtype=F32) / D_ROPE))
    ang = pos.astype(F32)[:, None] * inv[None, :]
    cos, sin = jnp.cos(ang), jnp.sin(ang)
    z = jnp.zeros((pos.shape[0], LANES - D_ROPE), F32)
    return jnp.concatenate([cos, cos, z, -sin, sin, z], axis=1)


def _mla_weights(w_in, g_q, g_kv, w_uq, w_uk, w_uv, w_o):
    half = D_ROPE // 2
    w_cq = w_in[:, :Q_LORA].astype(BF16)
    w_ckv = w_in[:, Q_LORA:Q_LORA + KV_LORA].astype(BF16)
    w_k = w_in[:, Q_LORA + KV_LORA:]
    zk = jnp.zeros((D_MODEL, LANES - D_ROPE), F32)
    w_kr = jnp.concatenate([w_k, zk, w_k[:, half:], w_k[:, :half], zk], axis=1).astype(BF16)
    wq = w_uq.reshape(Q_LORA, N_HEADS, D_NOPE + D_ROPE)
    wq_n = wq[:, :, :D_NOPE].reshape(Q_LORA, N_HEADS * D_NOPE).astype(BF16)
    r = wq[:, :, D_NOPE:]
    zq = jnp.zeros((Q_LORA, N_HEADS, LANES - D_ROPE), F32)
    wq_r = jnp.concatenate([r, zq], axis=2).reshape(Q_LORA, N_HEADS * LANES).astype(BF16)
    wq_s = jnp.concatenate([r[:, :, half:], r[:, :, :half], zq], axis=2).reshape(Q_LORA, N_HEADS * LANES).astype(BF16)
    return dict(w_cq=w_cq, w_ckv=w_ckv, w_kr=w_kr, g_q=g_q.reshape(1, -1), g_kv=g_kv.reshape(1, -1),
                wq_n=wq_n, wq_r=wq_r, wq_s=wq_s, w_uk=w_uk.astype(BF16), w_ukT=w_uk.T.astype(BF16),
                w_uv=w_uv.astype(BF16), w_o=w_o.astype(BF16))


def _mla_proj(x, cs, w, absorb, tn):
    n = x.shape[0]
    hw = N_HEADS * LANES
    row = lambda width: pl.BlockSpec((tn, width), lambda i: (i, 0))
    wa = w["w_ukT"] if absorb else w["w_uk"]
    wb = w["w_uv"]
    in_specs = [row(D_MODEL), _const_spec(w["w_cq"].shape), _const_spec(w["w_ckv"].shape),
                _const_spec(w["w_kr"].shape), _const_spec(w["g_q"].shape), _const_spec(w["g_kv"].shape),
                _const_spec(w["wq_n"].shape), _const_spec(w["wq_r"].shape), _const_spec(w["wq_s"].shape),
                row(2 * LANES), _const_spec(wa.shape), _const_spec(wb.shape)]
    if absorb:
        out_shape = [jax.ShapeDtypeStruct((n, KV_LORA), F32), jax.ShapeDtypeStruct((n, D_ROPE), F32),
                     jax.ShapeDtypeStruct((n, N_HEADS * KV_LORA), BF16), jax.ShapeDtypeStruct((n, hw), BF16)]
        out_specs = [row(KV_LORA), row(D_ROPE), row(N_HEADS * KV_LORA), row(hw)]
    else:
        out_shape = [jax.ShapeDtypeStruct((n, KV_LORA), F32), jax.ShapeDtypeStruct((n, D_ROPE), F32),
                     jax.ShapeDtypeStruct((n, hw), BF16), jax.ShapeDtypeStruct((n, hw), BF16),
                     jax.ShapeDtypeStruct((n, hw), BF16), jax.ShapeDtypeStruct((n, LANES), BF16),
                     jax.ShapeDtypeStruct((n, hw), BF16)]
        out_specs = [row(KV_LORA), row(D_ROPE), row(hw), row(hw), row(hw), row(LANES), row(hw)]
    return pl.pallas_call(
        functools.partial(_mla_proj_kernel, absorb),
        grid=(n // tn,), in_specs=in_specs, out_specs=out_specs, out_shape=out_shape,
        compiler_params=_cparams(("arbitrary",)), name="mla_proj_absorb" if absorb else "mla_proj",
    )(x, w["w_cq"], w["w_ckv"], w["w_kr"], w["g_q"], w["g_kv"], w["wq_n"], w["wq_r"], w["wq_s"], cs, wa, wb)


def _flash_kernel(qn_ref, qr_ref, kn_ref, kr_ref, v_ref, x_ref, wo_ref, g_ref, b_ref, o_ref,
                  m_ref, acc_ref):
    qi = pl.program_id(0)
    ki = pl.program_id(1)
    tq = qn_ref.shape[0]
    tk = kn_ref.shape[0]

    @pl.when(ki == 0)
    def _():
        m_ref[...] = jnp.full(m_ref.shape, NEG_BIG, F32)
        acc_ref[...] = jnp.zeros(acc_ref.shape, F32)

    def step(masked):
        kr = kr_ref[...]
        ones = jnp.ones((tk, LANES), BF16)
        if masked:
            keep = (lax.broadcasted_iota(jnp.int32, (tq, tk), 1)
                    <= lax.broadcasted_iota(jnp.int32, (tq, tk), 0))

        def scores(h):
            hs = slice(h * LANES, (h + 1) * LANES)
            q = jnp.concatenate([qn_ref[:, hs], qr_ref[:, hs]], axis=1)
            k = jnp.concatenate([kn_ref[:, hs], kr], axis=1)
            return _dot_nt(q, k)

        s_next = scores(0)
        for h in range(N_HEADS):
            s = s_next
            if h + 1 < N_HEADS:
                s_next = scores(h + 1)
            if masked:
                s = jnp.where(keep, s, -jnp.inf)
            hs = slice(h * LANES, (h + 1) * LANES)
            m_prev = m_ref[h]
            m_new = jnp.maximum(m_prev, jnp.max(s, -1, keepdims=True))
            p = jnp.exp2(s - jnp.concatenate([m_new] * (tk // LANES), axis=1))
            corr = jnp.exp2(m_prev - m_new)
            v1 = jnp.concatenate([v_ref[:, hs], ones], axis=1)
            acc_ref[h] = jnp.concatenate([corr, corr], axis=1) * acc_ref[h] + _dot(p.astype(BF16), v1)
            m_ref[h] = m_new

    @pl.when(ki < qi)
    def _():
        step(False)

    @pl.when(ki == qi)
    def _():
        step(True)
        o = jnp.concatenate([acc_ref[h, :, :D_V] / acc_ref[h, :, D_V:] for h in range(N_HEADS)], axis=1)
        z = ALPHA * x_ref[...] + _dot(o.astype(BF16), wo_ref[...])
        o_ref[...] = _layer_norm(z, g_ref[...], b_ref[...])


def _flash_prompt(qn, qr, kn, krp, v, x, w_o, g, b, tq):
    n = x.shape[0]
    hw = N_HEADS * LANES
    qspec = lambda width: pl.BlockSpec((tq, width), lambda i, j: (i, 0))
    kspec = lambda width: pl.BlockSpec((tq, width), lambda i, j: (jnp.minimum(i, j), 0))
    return pl.pallas_call(
        _flash_kernel,
        grid=(n // tq, n // tq),
        in_specs=[qspec(hw), qspec(hw), kspec(hw), kspec(LANES), kspec(hw), qspec(D_MODEL),
                  _const_spec(w_o.shape), _const_spec(g.shape), _const_spec(b.shape)],
        out_specs=qspec(D_MODEL),
        out_shape=jax.ShapeDtypeStruct((n, D_MODEL), F32),
        scratch_shapes=[pltpu.VMEM((N_HEADS, tq, LANES), F32), pltpu.VMEM((N_HEADS, tq, D_V + LANES), F32)],
        compiler_params=_cparams(("arbitrary", "arbitrary")), name="mla_flash_prompt",
    )(qn, qr, kn, krp, v, x, w_o, g, b)


PAGES_PER_STEP = 32
ROWS = 32


def _decode_kernel(pt_ref, ql_ref, qr_ref, cn_ref, rn_ref, *rest):
    lat_refs = rest[:PAGES_PER_STEP]
    rope_refs = rest[PAGES_PER_STEP:2 * PAGES_PER_STEP]
    o_ref, m_ref, l_ref, acc_ref, kcat_ref, rcat_ref = rest[2 * PAGES_PER_STEP:]
    c = pl.program_id(1)
    nc = pl.num_programs(1)

    @pl.when(c == 0)
    def _():
        m_ref[...] = jnp.full(m_ref.shape, NEG_BIG, F32)
        l_ref[...] = jnp.zeros(l_ref.shape, F32)
        acc_ref[...] = jnp.zeros(acc_ref.shape, F32)

    ql = ql_ref[...]
    qr = qr_ref[:, :D_ROPE]
    for p in range(PAGES_PER_STEP):
        kcat_ref[p * PAGE_SIZE:(p + 1) * PAGE_SIZE, :] = lat_refs[p][...].astype(BF16)
        rcat_ref[:, p * PAGE_SIZE:(p + 1) * PAGE_SIZE] = rope_refs[p][...].astype(BF16)
    kcat = kcat_ref[...]
    s = _dot_nt(ql, kcat) + _dot(qr, rcat_ref[...])
    m_prev = m_ref[...]
    m_new = jnp.maximum(m_prev, jnp.max(s, -1, keepdims=True))
    corr = jnp.exp(m_prev - m_new)
    pp = jnp.exp(s - m_new)
    m_ref[...] = m_new
    l_ref[...] = corr * l_ref[...] + jnp.sum(pp, -1, keepdims=True)
    acc_ref[...] = corr * acc_ref[...] + _dot(pp.astype(BF16), kcat)

    @pl.when(c == nc - 1)
    def _():
        qlf = ql.astype(F32)
        qrf = qr.astype(F32)
        cn = cn_ref[...].astype(BF16).astype(F32)
        rn = rn_ref[...].astype(BF16).astype(F32)
        t_of_row = lax.broadcasted_iota(jnp.int32, (ROWS, 1), 0) // N_HEADS
        n_new = cn.shape[0]
        s_new = []
        for j in range(n_new):
            sj = (jnp.sum(qlf * cn[j:j + 1, :], -1, keepdims=True)
                  + jnp.sum(qrf * rn[j:j + 1, :], -1, keepdims=True))
            s_new.append(jnp.where(j <= t_of_row, sj, -jnp.inf))
        m0 = m_ref[...]
        m1 = m0
        for sj in s_new:
            m1 = jnp.maximum(m1, sj)
        corr1 = jnp.exp(m0 - m1)
        l1 = corr1 * l_ref[...]
        a1 = corr1 * acc_ref[...]
        for j in range(n_new):
            pj = jnp.exp(s_new[j] - m1)
            l1 = l1 + pj
            a1 = a1 + pj.astype(BF16).astype(F32) * cn[j:j + 1, :]
        o_ref[...] = a1 / l1


def _decode(qlat, qr, ckv_new, kr_new, cache_lat, cache_kr, page_table):
    nb, n_pages = page_table.shape
    nc = n_pages // PAGES_PER_STEP
    n_new = ckv_new.shape[1]

    def page_spec(p, rows, width):
        return pl.BlockSpec((None, None, rows, width),
                            lambda b, c, pt: (0, pt[b, c * PAGES_PER_STEP + p], 0, 0))

    per_b = lambda rows, width: pl.BlockSpec((None, rows, width), lambda b, c, pt: (b, 0, 0))
    in_specs = ([per_b(ROWS, KV_LORA), per_b(ROWS, LANES), per_b(n_new, KV_LORA), per_b(n_new, D_ROPE)]
                + [page_spec(p, PAGE_SIZE, KV_LORA) for p in range(PAGES_PER_STEP)]
                + [page_spec(p, D_ROPE, PAGE_SIZE) for p in range(PAGES_PER_STEP)])
    keys = PAGES_PER_STEP * PAGE_SIZE
    grid_spec = pltpu.PrefetchScalarGridSpec(
        num_scalar_prefetch=1, grid=(nb, nc), in_specs=in_specs,
        out_specs=per_b(ROWS, KV_LORA),
        scratch_shapes=[pltpu.VMEM((ROWS, 1), F32), pltpu.VMEM((ROWS, 1), F32), pltpu.VMEM((ROWS, KV_LORA), F32),
                        pltpu.VMEM((keys, KV_LORA), BF16), pltpu.VMEM((D_ROPE, keys), BF16)])
    return pl.pallas_call(
        _decode_kernel, grid_spec=grid_spec,
        out_shape=jax.ShapeDtypeStruct((nb, ROWS, KV_LORA), F32),
        compiler_params=_cparams(("arbitrary", "arbitrary")), name="mla_decode",
    )(page_table, qlat, qr, ckv_new, kr_new, *([cache_lat] * PAGES_PER_STEP), *([cache_kr] * PAGES_PER_STEP))


def _sample_out_kernel(ol_ref, x_ref, wuv_ref, wo_ref, g_ref, b_ref, o_ref):
    olb = ol_ref[...].astype(BF16)
    o = jnp.concatenate(
        [_dot(olb[:, h * KV_LORA:(h + 1) * KV_LORA], wuv_ref[:, h * D_V:(h + 1) * D_V]) for h in range(N_HEADS)],
        axis=1)
    z = ALPHA * x_ref[...] + _dot(o.astype(BF16), wo_ref[...])
    o_ref[...] = _layer_norm(z, g_ref[...], b_ref[...])


def _sample_out(o_lat, x, w_uv, w_o, g, b):
    n = x.shape[0]
    args = (o_lat, x, w_uv, w_o, g, b)
    return pl.pallas_call(
        _sample_out_kernel, grid=(1,), in_specs=[_const_spec(a.shape) for a in args],
        out_specs=_const_spec((n, D_MODEL)), out_shape=jax.ShapeDtypeStruct((n, D_MODEL), F32),
        compiler_params=_cparams(("arbitrary",)), name="mla_sample_out",
    )(*args)


def _sort16_network():
    def merge(lo, hi, r):
        step = r * 2
        if step < hi - lo:
            yield from merge(lo, hi, step)
            yield from merge(lo + r, hi, step)
            yield from [(i, i + r) for i in range(lo + r, hi - r, step)]
        else:
            yield (lo, lo + r)

    def sort(lo, hi):
        if hi - lo >= 1:
            mid = lo + (hi - lo) // 2
            yield from sort(lo, mid)
            yield from sort(mid + 1, hi)
            yield from merge(lo, hi, 1)

    return tuple(sort(0, PEER_TOPK - 1))


_SORT16 = _sort16_network()


def _sort16_desc(v):
    v = list(v)
    for i, j in _SORT16:
        v[i], v[j] = jnp.maximum(v[i], v[j]), jnp.minimum(v[i], v[j])
    return v


def _bitonic_merge16_desc(v):
    v = list(v)
    d = PEER_TOPK // 2
    while d >= 1:
        for i in range(PEER_TOPK):
            if i & d == 0:
                v[i], v[i + d] = jnp.maximum(v[i], v[i + d]), jnp.minimum(v[i], v[i + d])
        d //= 2
    return v


def _merge_top16(x, y):
    return _bitonic_merge16_desc([jnp.maximum(x[k], y[PEER_TOPK - 1 - k]) for k in range(PEER_TOPK)])


def _top16_rows(s):
    v = _sort16_desc([s[SUBLANES * k:SUBLANES * (k + 1), :] for k in range(N_KEYS // SUBLANES)])
    for shift in (4, 2, 1):
        v = _merge_top16(v, [pltpu.roll(a, shift, 0) for a in v])
    return v


def _pair_region():
    return [(a, b) for a in range(PEER_TOPK) for b in range(PEER_TOPK) if (a + 1) * (b + 1) <= PEER_TOPK]


def _peer_route_kernel(x_ref, wqt_ref, keys_ref, s1_ref, e1_ref, s2_ref, e2_ref, tau_ref):
    tn = x_ref.shape[0]
    xb = x_ref[...].astype(BF16)
    qt = _dot_nt(wqt_ref[...], xb)
    sub = lax.broadcasted_iota(jnp.int32, (SUBLANES, tn), 0)
    tops = []
    packed = [[jnp.zeros((SUBLANES, tn), F32) for _ in range(PEER_TOPK)] for _ in range(2)]
    scores = []
    for h in range(PEER_HEADS):
        for half in range(2):
            c = 2 * h + half
            st = _dot(keys_ref[c], qt[c * HALF_KEY:(c + 1) * HALF_KEY, :].astype(BF16))
            scores.append(st)
            top = _top16_rows(st)
            tops.append(top[0])
            for k in range(PEER_TOPK):
                packed[half][k] = jnp.where(sub == h, top[k], packed[half][k])
    cands = [packed[0][a] + packed[1][b] for a, b in _pair_region()]
    pad = (-len(cands)) % PEER_TOPK
    cands += [jnp.full((SUBLANES, tn), -jnp.inf, F32)] * pad
    best = _sort16_desc(cands[:PEER_TOPK])
    for g in range(1, len(cands) // PEER_TOPK):
        best = _merge_top16(best, _sort16_desc(cands[g * PEER_TOPK:(g + 1) * PEER_TOPK]))
    tau = best[PEER_TOPK - 1]
    z = jnp.zeros((SUBLANES, tn), F32)
    for k in range(PEER_TOPK):
        z = z + jnp.exp(best[k] - best[0])
    inv_z = 1.0 / z
    tau_ref[...] = tau
    for h in range(PEER_HEADS):
        m1 = jnp.concatenate([tops[2 * h]] * (N_KEYS // SUBLANES), axis=0)
        m2 = jnp.concatenate([tops[2 * h + 1]] * (N_KEYS // SUBLANES), axis=0)
        s1_ref[h] = scores[2 * h]
        s2_ref[h] = scores[2 * h + 1]
        e1_ref[h] = jnp.exp(scores[2 * h] - m1) * inv_z[h:h + 1, :]
        e2_ref[h] = jnp.exp(scores[2 * h + 1] - m2)


def _peer_route(x, w_qt, keys, layer, tn):
    n = x.shape[0]
    big = jax.ShapeDtypeStruct((PEER_HEADS, N_KEYS, n), F32)
    bspec = pl.BlockSpec((PEER_HEADS, N_KEYS, tn), lambda i: (0, 0, i))
    return pl.pallas_call(
        _peer_route_kernel, grid=(n // tn,),
        in_specs=[pl.BlockSpec((tn, D_MODEL), lambda i: (i, 0)),
                  pl.BlockSpec((None,) + w_qt.shape[1:], lambda i: (layer, 0, 0)),
                  pl.BlockSpec((None,) + keys.shape[1:], lambda i: (layer, 0, 0, 0))],
        out_specs=[bspec, bspec, bspec, bspec, pl.BlockSpec((PEER_HEADS, tn), lambda i: (0, i))],
        out_shape=[big, big, big, big, jax.ShapeDtypeStruct((PEER_HEADS, n), F32)],
        compiler_params=_cparams(("arbitrary",)), name="peer_route",
    )(x, w_qt, keys)


PEER_TE = SUBLANES * N_KEYS
PEER_JB = 2


def _peer_dense_kernel(x_ref, u_ref, vt_ref, s1_ref, e1_ref, s2_ref, e2_ref, tau_ref, g_ref, b_ref,
                       o_ref, acc_ref, h_ref, gate_ref, xb_ref, s1b_ref, e1b_ref, taub_ref):
    j = pl.program_id(1)
    last = pl.num_programs(1) - 1
    tm = x_ref.shape[0]

    def weighted_prev():
        return (_gelu_tanh(h_ref[...]) * gate_ref[...]).astype(BF16)

    @pl.when(j == 0)
    def _():
        acc_ref[...] = jnp.zeros(acc_ref.shape, F32)
        h_ref[...] = jnp.zeros(h_ref.shape, F32)
        gate_ref[...] = jnp.zeros(gate_ref.shape, F32)
        xb_ref[...] = x_ref[...].astype(BF16)
        for h in range(PEER_HEADS):
            taub_ref[h * SUBLANES:(h + 1) * SUBLANES, :] = jnp.broadcast_to(tau_ref[h:h + 1, :], (SUBLANES, tm))

    @pl.when(j < last)
    def _():
        acc_ref[...] += _dot(vt_ref[...], weighted_prev())
        h_ref[...] = _dot_nt(u_ref[...], xb_ref[...])
        for h in range(PEER_HEADS):
            for ii in range(SUBLANES):
                brow = slice((h * SUBLANES + ii) * SUBLANES, (h * SUBLANES + ii + 1) * SUBLANES)
                s1b_ref[brow, :] = jnp.broadcast_to(s1_ref[h, ii:ii + 1, :], (SUBLANES, tm))
                e1b_ref[brow, :] = jnp.broadcast_to(e1_ref[h, ii:ii + 1, :], (SUBLANES, tm))

        def chunk(k, carry):
            j0 = k * (PEER_JB * SUBLANES)
            for lc in range(tm // LANES):
                ls = slice(lc * LANES, (lc + 1) * LANES)
                gates = [[jnp.zeros((SUBLANES, LANES), F32) for _ in range(SUBLANES)] for _ in range(PEER_JB)]
                for h in range(PEER_HEADS):
                    tau = taub_ref[h * SUBLANES:(h + 1) * SUBLANES, ls]
                    jrows = [pl.ds(pl.multiple_of(j0 + t * SUBLANES, SUBLANES), SUBLANES) for t in range(PEER_JB)]
                    s2 = [s2_ref[h, jrows[t], ls] for t in range(PEER_JB)]
                    e2 = [e2_ref[h, jrows[t], ls] for t in range(PEER_JB)]
                    for ii in range(SUBLANES):
                        brow = slice((h * SUBLANES + ii) * SUBLANES, (h * SUBLANES + ii + 1) * SUBLANES)
                        s1 = s1b_ref[brow, ls]
                        e1 = e1b_ref[brow, ls]
                        for t in range(PEER_JB):
                            gates[t][ii] = gates[t][ii] + jnp.where(s2[t] + s1 >= tau, e2[t] * e1, 0.0)
                for t in range(PEER_JB):
                    for ii in range(SUBLANES):
                        r0 = pl.multiple_of(ii * N_KEYS + j0 + t * SUBLANES, SUBLANES)
                        gate_ref[pl.ds(r0, SUBLANES), ls] = gates[t][ii]
            return carry

        lax.fori_loop(0, N_KEYS // (PEER_JB * SUBLANES), chunk, 0)

    @pl.when(j == last)
    def _():
        acc = acc_ref[...] + _dot(vt_ref[...], weighted_prev())
        z = ALPHA * x_ref[...] + acc.T
        o_ref[...] = _layer_norm(z, g_ref[...], b_ref[...])


def _peer_dense(x, u, vt, s1, e1, s2, e2, tau, g, b, layer, tm):
    n = x.shape[0]
    nt = N_EXPERTS // PEER_TE
    full = pl.BlockSpec((PEER_HEADS, N_KEYS, tm), lambda i, j: (0, 0, i))
    part = pl.BlockSpec((PEER_HEADS, SUBLANES, tm), lambda i, j: (0, jnp.minimum(j, nt - 1), i))
    nb = PEER_HEADS * SUBLANES * SUBLANES
    return pl.pallas_call(
        _peer_dense_kernel, grid=(n // tm, nt + 1),
        in_specs=[pl.BlockSpec((tm, D_MODEL), lambda i, j: (i, 0)),
                  pl.BlockSpec((None, PEER_TE, D_MODEL), lambda i, j: (layer, jnp.minimum(j, nt - 1), 0)),
                  pl.BlockSpec((None, D_MODEL, PEER_TE), lambda i, j: (layer, 0, jnp.maximum(j - 1, 0))),
                  part, part, full, full,
                  pl.BlockSpec((PEER_HEADS, tm), lambda i, j: (0, i)),
                  _const_spec(g.shape), _const_spec(b.shape)],
        out_specs=pl.BlockSpec((tm, D_MODEL), lambda i, j: (i, 0)),
        out_shape=jax.ShapeDtypeStruct((n, D_MODEL), F32),
        scratch_shapes=[pltpu.VMEM((D_MODEL, tm), F32), pltpu.VMEM((PEER_TE, tm), F32),
                        pltpu.VMEM((PEER_TE, tm), F32), pltpu.VMEM((tm, D_MODEL), BF16),
                        pltpu.VMEM((nb, tm), F32), pltpu.VMEM((nb, tm), F32),
                        pltpu.VMEM((PEER_HEADS * SUBLANES, tm), F32)],
        compiler_params=_cparams(("arbitrary", "arbitrary")), name="peer_dense",
    )(x, u, vt, s1, e1, s2, e2, tau, g, b)


def _peer_weights(w_q, sub_keys, exp_u, exp_v):
    nl = w_q.shape[0]
    return dict(w_qt=w_q.transpose(0, 2, 1).astype(BF16),
                keys=sub_keys.reshape(nl, 2 * PEER_HEADS, N_KEYS, HALF_KEY).astype(BF16),
                u=exp_u.astype(BF16), vt=exp_v.transpose(0, 2, 1).astype(BF16))


def _peer_ln(x, w, layer, g, b, tn_route, tm):
    s1, e1, s2, e2, tau = _peer_route(x, w["w_qt"], w["keys"], layer, tn_route)
    return _peer_dense(x, w["u"], w["vt"], s1, e1, s2, e2, tau, g, b, layer, tm)


def _s5_disc_kernel(are_ref, aim_ref, ldt_ref, bre_ref, bim_ref, abre_ref, abim_ref, bbre_ref, bbim_ref):
    lr = jnp.minimum(are_ref[...], -1e-4)
    li = aim_ref[...]
    dt = jnp.exp(ldt_ref[...])
    mag = jnp.exp(lr * dt)
    ab_re = mag * jnp.cos(li * dt)
    ab_im = mag * jnp.sin(li * dt)
    den = lr * lr + li * li
    nr, ni = ab_re - 1.0, ab_im
    f_re = (nr * lr + ni * li) / den
    f_im = (ni * lr - nr * li) / den
    br, bi = bre_ref[...], bim_ref[...]
    abre_ref[...] = ab_re
    abim_ref[...] = ab_im
    bbre_ref[...] = f_re * br - f_im * bi
    bbim_ref[...] = f_re * bi + f_im * br


def _s5_weights(w_in, a_re, a_im, log_dt, b_re, b_im, c_re, c_im, d_skip, w_glu, w_o):
    rep = lambda a: jnp.repeat(a, GROUP_SIZE, axis=0)
    bt = lambda b: b.transpose(0, 2, 1).reshape(D_MODEL, STATE)
    args = (rep(a_re), rep(a_im), rep(log_dt.reshape(N_GROUPS, 1)), bt(b_re), bt(b_im))
    sd = jax.ShapeDtypeStruct((D_MODEL, STATE), F32)
    ab_re, ab_im, bb_re, bb_im = pl.pallas_call(
        _s5_disc_kernel, grid=(1,), in_specs=[_const_spec(a.shape) for a in args],
        out_specs=[_const_spec(sd.shape)] * 4, out_shape=[sd] * 4,
        compiler_params=_cparams(("arbitrary",)), name="s5_discretize")(*args)
    nblk = 4
    gl = N_GROUPS // nblk
    eye = jnp.eye(gl, dtype=F32)

    def bdiag_in(bb):
        t = bb.reshape(nblk, gl, GROUP_SIZE, STATE)
        return jnp.einsum('kgpn,gh->kgphn', t, eye).reshape(nblk, gl * GROUP_SIZE, gl * STATE).astype(BF16)

    def bdiag_out(c):
        t = c.reshape(nblk, gl, GROUP_SIZE, STATE)
        return jnp.einsum('kgpn,gh->kgnhp', t, eye).reshape(nblk, gl * STATE, gl * GROUP_SIZE)

    c_cat = jnp.concatenate([bdiag_out(c_re), -bdiag_out(c_im)], axis=1).astype(BF16)
    return dict(w_in=w_in.astype(BF16), b_re=bdiag_in(bb_re), b_im=bdiag_in(bb_im),
                a_re=ab_re[::GROUP_SIZE].reshape(1, N_STATE), a_im=ab_im[::GROUP_SIZE].reshape(1, N_STATE),
                c_cat=c_cat, d=d_skip.reshape(1, D_MODEL), w_glu=w_glu.astype(BF16), w_o=w_o.astype(BF16))


S5_LW = 2 * LANES
S5_NBLK = 4


def _cmul(ar, ai, br, bi):
    return ar * br - ai * bi, ar * bi + ai * br


def _s5_scan_kernel(seg, x_ref, win_ref, bre_ref, bim_ref, are_ref, aim_ref, ccat_ref, d_ref, *rest):
    if seg:
        s0re_ref, s0im_ref, g_ref, ore_ref, oim_ref, sre_ref, sim_ref, tab_ref = rest
    else:
        g_ref, ore_ref, oim_ref, sre_ref, sim_ref, tab_ref, car_ref = rest
    i = pl.program_id(0)
    tl = x_ref.shape[0]
    period = seg if seg else SUBLANES

    @pl.when(i == 0)
    def _():
        ar = jnp.broadcast_to(are_ref[...], (SUBLANES, N_STATE))
        ai = jnp.broadcast_to(aim_ref[...], (SUBLANES, N_STATE))
        rr = lax.broadcasted_iota(jnp.int32, (SUBLANES, N_STATE), 0) % period
        pr, pi = ar, ai
        powers = [(pr, pi)]
        for _ in range(SUBLANES - 1):
            pr, pi = _cmul(pr, pi, ar, ai)
            powers.append((pr, pi))
        for k, d in enumerate((1, 2, 4)):
            tab_ref[2 * k] = jnp.where(rr >= d, powers[d - 1][0], 0.0)
            tab_ref[2 * k + 1] = jnp.where(rr >= d, powers[d - 1][1], 0.0)
        cr = jnp.zeros((SUBLANES, N_STATE), F32)
        ci = jnp.zeros((SUBLANES, N_STATE), F32)
        for p in range(SUBLANES):
            cr = jnp.where(rr == p, powers[p][0], cr)
            ci = jnp.where(rr == p, powers[p][1], ci)
        tab_ref[6] = cr
        tab_ref[7] = ci
        if not seg:
            car_ref[...] = jnp.zeros(car_ref.shape, F32)

    u = _dot(x_ref[...].astype(BF16), win_ref[...])
    ub = u.astype(BF16)
    kin = D_MODEL // S5_NBLK
    kst = N_STATE // S5_NBLK
    for k in range(S5_NBLK):
        sre_ref[:, k * kst:(k + 1) * kst] = _dot(ub[:, k * kin:(k + 1) * kin], bre_ref[k])
        sim_ref[:, k * kst:(k + 1) * kst] = _dot(ub[:, k * kin:(k + 1) * kin], bim_ref[k])

    steps = tuple(d for d in (1, 2, 4) if d < period)

    def body(gi, carry):
        rows = pl.ds(pl.multiple_of(gi * SUBLANES, SUBLANES), SUBLANES)
        for lc in range(N_STATE // S5_LW):
            ls = slice(lc * S5_LW, (lc + 1) * S5_LW)
            xr = sre_ref[rows, ls]
            xi = sim_ref[rows, ls]
            for d in steps:
                k = (1, 2, 4).index(d)
                yr, yi = _cmul(tab_ref[2 * k, :, ls], tab_ref[2 * k + 1, :, ls],
                               pltpu.roll(xr, d, 0), pltpu.roll(xi, d, 0))
                xr, xi = xr + yr, xi + yi
            if seg:
                cr, ci = s0re_ref[rows, ls], s0im_ref[rows, ls]
            else:
                cr, ci = car_ref[0, :, ls], car_ref[1, :, ls]
            yr, yi = _cmul(tab_ref[6, :, ls], tab_ref[7, :, ls], cr, ci)
            xr, xi = xr + yr, xi + yi
            sre_ref[rows, ls] = xr
            sim_ref[rows, ls] = xi
            if not seg:
                car_ref[0, :, ls] = jnp.broadcast_to(xr[SUBLANES - 1:SUBLANES, :], (SUBLANES, S5_LW))
                car_ref[1, :, ls] = jnp.broadcast_to(xi[SUBLANES - 1:SUBLANES, :], (SUBLANES, S5_LW))
        return carry

    lax.fori_loop(0, tl // SUBLANES, body, 0)

    if seg:
        ore_ref[...] = sre_ref[...]
        oim_ref[...] = sim_ref[...]
    else:
        ore_ref[...] = car_ref[0, 0:1, :]
        oim_ref[...] = car_ref[1, 0:1, :]

    kout = D_MODEL // S5_NBLK
    ys = []
    for k in range(S5_NBLK):
        st = jnp.concatenate([sre_ref[:, k * kst:(k + 1) * kst], sim_ref[:, k * kst:(k + 1) * kst]], axis=1)
        ys.append(_dot(st.astype(BF16), ccat_ref[k]))
    y = jnp.concatenate(ys, axis=1) + d_ref[...] * u
    g_ref[...] = _gelu_tanh(y).astype(BF16)


def _s5_scan(x, w, tl, s0=None):
    n = x.shape[0]
    seg = 0 if s0 is None else 4
    row = lambda width: pl.BlockSpec((tl, width), lambda i: (i, 0))
    wargs = (w["w_in"], w["b_re"], w["b_im"], w["a_re"], w["a_im"], w["c_cat"], w["d"])
    in_specs = [row(D_MODEL)] + [_const_spec(a.shape) for a in wargs]
    args = (x,) + wargs
    scratch = [pltpu.VMEM((tl, N_STATE), F32), pltpu.VMEM((tl, N_STATE), F32),
               pltpu.VMEM((8, SUBLANES, N_STATE), F32)]
    if seg:
        in_specs += [row(N_STATE), row(N_STATE)]
        args += tuple(s0)
        st_shape = jax.ShapeDtypeStruct((n, N_STATE), F32)
        st_spec = row(N_STATE)
    else:
        scratch.append(pltpu.VMEM((2, SUBLANES, N_STATE), F32))
        st_shape = jax.ShapeDtypeStruct((1, N_STATE), F32)
        st_spec = _const_spec((1, N_STATE))
    return pl.pallas_call(
        functools.partial(_s5_scan_kernel, seg), grid=(n // tl,), in_specs=in_specs,
        out_specs=[row(D_MODEL), st_spec, st_spec],
        out_shape=[jax.ShapeDtypeStruct((n, D_MODEL), BF16), st_shape, st_shape],
        scratch_shapes=scratch,
        compiler_params=_cparams(("arbitrary",)), name="s5_scan_seg" if seg else "s5_scan",
    )(*args)


def _glu_out_kernel(gin_ref, x_ref, wglu_ref, wo_ref, g_ref, b_ref, o_ref):
    ga = _dot(gin_ref[...], wglu_ref[...])
    hid = ga[:, :D_MODEL] * jax.nn.sigmoid(ga[:, D_MODEL:])
    z = ALPHA * x_ref[...] + _dot(hid.astype(BF16), wo_ref[...])
    o_ref[...] = _layer_norm(z, g_ref[...], b_ref[...])


def _glu_out(gin, x, w_glu, w_o, g, b, tn):
    n = x.shape[0]
    row = lambda: pl.BlockSpec((tn, D_MODEL), lambda i: (i, 0))
    return pl.pallas_call(
        _glu_out_kernel, grid=(n // tn,),
        in_specs=[row(), row(), _const_spec(w_glu.shape), _const_spec(w_o.shape),
                  _const_spec(g.shape), _const_spec(b.shape)],
        out_specs=row(), out_shape=jax.ShapeDtypeStruct((n, D_MODEL), F32),
        compiler_params=_cparams(("arbitrary",)), name="s5_glu_out",
    )(gin, x, w_glu, w_o, g, b)


def _tiles(n):
    fit = lambda pref: pref if n % pref == 0 else n
    return dict(proj=fit(512), flash=fit(512), scan=fit(256), seg=fit(128), glu=fit(512),
                route=fit(256), dense=fit(512))


def kernel(x_prompt, x_sample, cache_kv_latent, cache_k_rope, state_ssm_re, state_ssm_im, page_table, mla_w_in, mla_g_q, mla_g_kv, mla_w_uq, mla_w_uk, mla_w_uv, mla_w_o, ssm_w_in, ssm_a_re, ssm_a_im, ssm_log_dt, ssm_b_re, ssm_b_im, ssm_c_re, ssm_c_im, ssm_d, ssm_w_glu, ssm_w_o, peer_w_q, peer_sub_keys, peer_u, peer_v, ln_g, ln_b):
    bp, lp, _ = x_prompt.shape
    bd, td, _ = x_sample.shape
    assert bp == 1 and td * N_HEADS == ROWS
    npr, nsm = bp * lp, bd * td
    yp = x_prompt.reshape(npr, D_MODEL)
    ys = x_sample.reshape(nsm, D_MODEL)
    lng = lambda layer, k: ln_g[layer, k].reshape(1, D_MODEL)
    lnb = lambda layer, k: ln_b[layer, k].reshape(1, D_MODEL)
    tp, ts = _tiles(npr), _tiles(nsm)
    pw = _peer_weights(peer_w_q, peer_sub_keys, peer_u, peer_v)

    outs = {}
    for layer in range(DEPTH):
        j = layer // 2
        if layer % 2 == 0:
            w = _mla_weights(mla_w_in[j], mla_g_q[j], mla_g_kv[j], mla_w_uq[j], mla_w_uk[j], mla_w_uv[j], mla_w_o[j])
            cs_p = _rope_cs(jnp.arange(lp))
            ckv_p, kr_p, qn, qr, kn, krp, v = _mla_proj(yp, cs_p, w, False, tp["proj"])
            yp = _flash_prompt(qn, qr, kn, krp, v, yp, w["w_o"], lng(layer, 0), lnb(layer, 0), tp["flash"])
            cs_s = _rope_cs(jnp.tile(PAST_LEN + jnp.arange(td), bd))
            ckv_s, kr_s, qlat, qr_s = _mla_proj(ys, cs_s, w, True, nsm)
            o_lat = _decode(qlat.reshape(bd, ROWS, KV_LORA), qr_s.reshape(bd, ROWS, LANES),
                            ckv_s.reshape(bd, td, KV_LORA), kr_s.reshape(bd, td, D_ROPE),
                            cache_kv_latent[j:j + 1], jnp.swapaxes(cache_k_rope[j:j + 1], 2, 3), page_table)
            ys = _sample_out(o_lat.reshape(nsm, N_HEADS * KV_LORA), ys, w["w_uv"], w["w_o"],
                             lng(layer, 0), lnb(layer, 0))
            outs.setdefault("p_lat", []).append(ckv_p.reshape(bp, lp, KV_LORA))
            outs.setdefault("p_kr", []).append(kr_p.reshape(bp, lp, D_ROPE))
            outs.setdefault("s_lat", []).append(ckv_s.reshape(bd, td, KV_LORA))
            outs.setdefault("s_kr", []).append(kr_s.reshape(bd, td, D_ROPE))
        else:
            w = _s5_weights(ssm_w_in[j], ssm_a_re[j], ssm_a_im[j], ssm_log_dt[j], ssm_b_re[j], ssm_b_im[j],
                            ssm_c_re[j], ssm_c_im[j], ssm_d[j], ssm_w_glu[j], ssm_w_o[j])
            g_p, sre_p, sim_p = _s5_scan(yp, w, tp["scan"])
            yp = _glu_out(g_p, yp, w["w_glu"], w["w_o"], lng(layer, 0), lnb(layer, 0), tp["glu"])
            s0 = (jnp.repeat(state_ssm_re[j].reshape(bd, N_STATE), td, axis=0),
                  jnp.repeat(state_ssm_im[j].reshape(bd, N_STATE), td, axis=0))
            g_s, sre_s, sim_s = _s5_scan(ys, w, ts["seg"], s0)
            ys = _glu_out(g_s, ys, w["w_glu"], w["w_o"], lng(layer, 0), lnb(layer, 0), ts["glu"])
            outs.setdefault("p_sre", []).append(sre_p.reshape(bp, N_GROUPS, STATE))
            outs.setdefault("p_sim", []).append(sim_p.reshape(bp, N_GROUPS, STATE))
            outs.setdefault("s_sre", []).append(sre_s[td - 1::td].reshape(bd, N_GROUPS, STATE))
            outs.setdefault("s_sim", []).append(sim_s[td - 1::td].reshape(bd, N_GROUPS, STATE))
        yp = _peer_ln(yp, pw, layer, lng(layer, 1), lnb(layer, 1), tp["route"], tp["dense"])
        ys = _peer_ln(ys, pw, layer, lng(layer, 1), lnb(layer, 1), ts["route"], ts["dense"])

    return (yp.reshape(bp, lp, D_MODEL), ys.reshape(bd, td, D_MODEL),
            jnp.stack(outs["p_lat"]), jnp.stack(outs["p_kr"]), jnp.stack(outs["p_sre"]), jnp.stack(outs["p_sim"]),
            jnp.stack(outs["s_lat"]), jnp.stack(outs["s_kr"]), jnp.stack(outs["s_sre"]), jnp.stack(outs["s_sim"]))
```

```python
import functools
import math

import jax
import jax.numpy as jnp
from jax import lax
from jax.experimental import pallas as pl
from jax.experimental.pallas import tpu as pltpu

F32 = jnp.float32
BF16 = jnp.bfloat16

D_MODEL = 1024
DEPTH = 2
PAST_LEN = 16384
PAGE_SIZE = 128

N_HEADS = 8
D_NOPE = 128
D_ROPE = 64
D_V = 128
Q_LORA = 384
KV_LORA = 256
ROPE_THETA = 10000.0
ATTN_SCALE = 1.0 / math.sqrt(D_NOPE + D_ROPE)

GROUP_SIZE = 16
N_GROUPS = D_MODEL // GROUP_SIZE
STATE = 64
N_STATE = N_GROUPS * STATE

PEER_HEADS = 8
N_KEYS = 128
N_EXPERTS = N_KEYS * N_KEYS
HALF_KEY = 128
PEER_TOPK = 16

ALPHA = (2 * DEPTH) ** 0.25
LN_EPS = 1e-5
RMS_EPS = 1e-6

LANES = 128
SUBLANES = 8
VMEM_LIMIT = 56 * 1024 * 1024

NEG_BIG = -1e30


def _cparams(sem):
    return pltpu.CompilerParams(dimension_semantics=sem, vmem_limit_bytes=VMEM_LIMIT)


def _dot(a, b):
    return jnp.dot(a, b, preferred_element_type=F32)


def _dot_nt(a, b):
    return lax.dot_general(a, b, (((1,), (1,)), ((), ())), preferred_element_type=F32)


def _layer_norm(z, g, b):
    mu = jnp.mean(z, -1, keepdims=True)
    zc = z - mu
    var = jnp.mean(zc * zc, -1, keepdims=True)
    return zc * lax.rsqrt(var + LN_EPS) * g + b


def _rms_norm(x, g):
    return x * lax.rsqrt(jnp.mean(x * x, -1, keepdims=True) + RMS_EPS) * g


def _gelu_tanh(x):
    c = math.sqrt(2.0 / math.pi)
    return x * (0.5 * (1.0 + jnp.tanh(c * (x + 0.044715 * (x * x * x)))))


def _const_spec(shape):
    n = len(shape)
    return pl.BlockSpec(shape, lambda *_: (0,) * n)


def _mla_proj_kernel(absorb, x_ref, wcq_ref, wckv_ref, wkr_ref, gq_ref, gkv_ref,
                     wuqn_ref, wuqr_ref, wuqs_ref, cs_ref, wa_ref, wb_ref, *out_refs):
    xb = x_ref[...].astype(BF16)
    c_q = _rms_norm(_dot(xb, wcq_ref[...]), gq_ref[...])
    c_kv = _rms_norm(_dot(xb, wckv_ref[...]), gkv_ref[...])
    hk = _dot(xb, wkr_ref[...])
    cos = cs_ref[:, :LANES]
    sin = cs_ref[:, LANES:]
    kr = hk[:, :LANES] * cos + hk[:, LANES:] * sin
    cqb = c_q.astype(BF16)
    cos8 = jnp.concatenate([cos] * N_HEADS, axis=1)
    sin8 = jnp.concatenate([sin] * N_HEADS, axis=1)
    qscale = ATTN_SCALE if absorb else ATTN_SCALE * math.log2(math.e)
    qn = _dot(cqb, wuqn_ref[...]) * qscale
    qr = (_dot(cqb, wuqr_ref[...]) * cos8 + _dot(cqb, wuqs_ref[...]) * sin8) * qscale
    ckvb = c_kv.astype(BF16)
    if absorb:
        ckv_ref, kr_ref, qlat_ref, qr_ref = out_refs
        qnb = qn.astype(BF16)
        for h in range(N_HEADS):
            qlat_ref[:, h * KV_LORA:(h + 1) * KV_LORA] = _dot(
                qnb[:, h * D_NOPE:(h + 1) * D_NOPE], wa_ref[h * D_NOPE:(h + 1) * D_NOPE, :]).astype(BF16)
    else:
        ckv_ref, kr_ref, qn_ref, qr_ref, kn_ref, krp_ref, v_ref = out_refs
        qn_ref[...] = qn.astype(BF16)
        kn_ref[...] = _dot(ckvb, wa_ref[...]).astype(BF16)
        v_ref[...] = _dot(ckvb, wb_ref[...]).astype(BF16)
        krp_ref[...] = kr.astype(BF16)
    ckv_ref[...] = c_kv
    kr_ref[...] = kr[:, :D_ROPE]
    qr_ref[...] = qr.astype(BF16)


def _rope_cs(pos):
    inv = 1.0 / (ROPE_THETA ** (jnp.arange(0, D_ROPE, 2, dtype=F32) / D_ROPE))
    ang = pos.astype(F32)[:, None] * inv[None, :]
    cos, sin = jnp.cos(ang), jnp.sin(ang)
    z = jnp.zeros((pos.shape[0], LANES - D_ROPE), F32)
    return jnp.concatenate([cos, cos, z, -sin, sin, z], axis=1)


def _mla_weights(w_in, g_q, g_kv, w_uq, w_uk, w_uv, w_o):
    half = D_ROPE // 2
    w_cq = w_in[:, :Q_LORA].astype(BF16)
    w_ckv = w_in[:, Q_LORA:Q_LORA + KV_LORA].astype(BF16)
    w_k = w_in[:, Q_LORA + KV_LORA:]
    zk = jnp.zeros((D_MODEL, LANES - D_ROPE), F32)
    w_kr = jnp.concatenate([w_k, zk, w_k[:, half:], w_k[:, :half], zk], axis=1).astype(BF16)
    wq = w_uq.reshape(Q_LORA, N_HEADS, D_NOPE + D_ROPE)
    wq_n = wq[:, :, :D_NOPE].reshape(Q_LORA, N_HEADS * D_NOPE).astype(BF16)
    r = wq[:, :, D_NOPE:]
    zq = jnp.zeros((Q_LORA, N_HEADS, LANES - D_ROPE), F32)
    wq_r = jnp.concatenate([r, zq], axis=2).reshape(Q_LORA, N_HEADS * LANES).astype(BF16)
    wq_s = jnp.concatenate([r[:, :, half:], r[:, :, :half], zq], axis=2).reshape(Q_LORA, N_HEADS * LANES).astype(BF16)
    return dict(w_cq=w_cq, w_ckv=w_ckv, w_kr=w_kr, g_q=g_q.reshape(1, -1), g_kv=g_kv.reshape(1, -1),
                wq_n=wq_n, wq_r=wq_r, wq_s=wq_s, w_uk=w_uk.astype(BF16), w_ukT=w_uk.T.astype(BF16),
                w_uv=w_uv.astype(BF16), w_o=w_o.astype(BF16))


def _mla_proj(x, cs, w, absorb, tn):
    n = x.shape[0]
    hw = N_HEADS * LANES
    row = lambda width: pl.BlockSpec((tn, width), lambda i: (i, 0))
    wa = w["w_ukT"] if absorb else w["w_uk"]
    wb = w["w_uv"]
    in_specs = [row(D_MODEL), _const_spec(w["w_cq"].shape), _const_spec(w["w_ckv"].shape),
                _const_spec(w["w_kr"].shape), _const_spec(w["g_q"].shape), _const_spec(w["g_kv"].shape),
                _const_spec(w["wq_n"].shape), _const_spec(w["wq_r"].shape), _const_spec(w["wq_s"].shape),
                row(2 * LANES), _const_spec(wa.shape), _const_spec(wb.shape)]
    if absorb:
        out_shape = [jax.ShapeDtypeStruct((n, KV_LORA), F32), jax.ShapeDtypeStruct((n, D_ROPE), F32),
                     jax.ShapeDtypeStruct((n, N_HEADS * KV_LORA), BF16), jax.ShapeDtypeStruct((n, hw), BF16)]
        out_specs = [row(KV_LORA), row(D_ROPE), row(N_HEADS * KV_LORA), row(hw)]
    else:
        out_shape = [jax.ShapeDtypeStruct((n, KV_LORA), F32), jax.ShapeDtypeStruct((n, D_ROPE), F32),
                     jax.ShapeDtypeStruct((n, hw), BF16), jax.ShapeDtypeStruct((n, hw), BF16),
                     jax.ShapeDtypeStruct((n, hw), BF16), jax.ShapeDtypeStruct((n, LANES), BF16),
                     jax.ShapeDtypeStruct((n, hw), BF16)]
        out_specs = [row(KV_LORA), row(D_ROPE), row(hw), row(hw), row(hw), row(LANES), row(hw)]
    return pl.pallas_call(
        functools.partial(_mla_proj_kernel, absorb),
        grid=(n // tn,), in_specs=in_specs, out_specs=out_specs, out_shape=out_shape,
        compiler_params=_cparams(("arbitrary",)), name="mla_proj_absorb" if absorb else "mla_proj",
    )(x, w["w_cq"], w["w_ckv"], w["w_kr"], w["g_q"], w["g_kv"], w["wq_n"], w["wq_r"], w["wq_s"], cs, wa, wb)


def _flash_kernel(qn_ref, qr_ref, kn_ref, kr_ref, v_ref, x_ref, wo_ref, g_ref, b_ref, o_ref,
                  m_ref, acc_ref):
    qi = pl.program_id(0)
    ki = pl.program_id(1)
    tq = qn_ref.shape[0]
    tk = kn_ref.shape[0]

    @pl.when(ki == 0)
    def _():
        m_ref[...] = jnp.full(m_ref.shape, NEG_BIG, F32)
        acc_ref[...] = jnp.zeros(acc_ref.shape, F32)

    def step(masked):
        kr = kr_ref[...]
        ones = jnp.ones((tk, LANES), BF16)
        if masked:
            keep = (lax.broadcasted_iota(jnp.int32, (tq, tk), 1)
                    <= lax.broadcasted_iota(jnp.int32, (tq, tk), 0))

        def scores(h):
            hs = slice(h * LANES, (h + 1) * LANES)
            q = jnp.concatenate([qn_ref[:, hs], qr_ref[:, hs]], axis=1)
            k = jnp.concatenate([kn_ref[:, hs], kr], axis=1)
            return _dot_nt(q, k)

        s_next = scores(0)
        for h in range(N_HEADS):
            s = s_next
            if h + 1 < N_HEADS:
                s_next = scores(h + 1)
            if masked:
                s = jnp.where(keep, s, -jnp.inf)
            hs = slice(h * LANES, (h + 1) * LANES)
            m_prev = m_ref[h]
            m_new = jnp.maximum(m_prev, jnp.max(s, -1, keepdims=True))
            p = jnp.exp2(s - jnp.concatenate([m_new] * (tk // LANES), axis=1))
            corr = jnp.exp2(m_prev - m_new)
            v1 = jnp.concatenate([v_ref[:, hs], ones], axis=1)
            acc_ref[h] = jnp.concatenate([corr, corr], axis=1) * acc_ref[h] + _dot(p.astype(BF16), v1)
            m_ref[h] = m_new

    @pl.when(ki < qi)
    def _():
        step(False)

    @pl.when(ki == qi)
    def _():
        step(True)
        o = jnp.concatenate([acc_ref[h, :, :D_V] / acc_ref[h, :, D_V:] for h in range(N_HEADS)], axis=1)
        z = ALPHA * x_ref[...] + _dot(o.astype(BF16), wo_ref[...])
        o_ref[...] = _layer_norm(z, g_ref[...], b_ref[...])


def _flash_prompt(qn, qr, kn, krp, v, x, w_o, g, b, tq):
    n = x.shape[0]
    hw = N_HEADS * LANES
    qspec = lambda width: pl.BlockSpec((tq, width), lambda i, j: (i, 0))
    kspec = lambda width: pl.BlockSpec((tq, width), lambda i, j: (jnp.minimum(i, j), 0))
    return pl.pallas_call(
        _flash_kernel,
        grid=(n // tq, n // tq),
        in_specs=[qspec(hw), qspec(hw), kspec(hw), kspec(LANES), kspec(hw), qspec(D_MODEL),
                  _const_spec(w_o.shape), _const_spec(g.shape), _const_spec(b.shape)],
        out_specs=qspec(D_MODEL),
        out_shape=jax.ShapeDtypeStruct((n, D_MODEL), F32),
        scratch_shapes=[pltpu.VMEM((N_HEADS, tq, LANES), F32), pltpu.VMEM((N_HEADS, tq, D_V + LANES), F32)],
        compiler_params=_cparams(("arbitrary", "arbitrary")), name="mla_flash_prompt",
    )(qn, qr, kn, krp, v, x, w_o, g, b)


PAGES_PER_STEP = 32
ROWS = 32


def _decode_kernel(pt_ref, ql_ref, qr_ref, cn_ref, rn_ref, lat_hbm, rope_hbm, o_ref,
                   m_ref, l_ref, acc_ref, kcat_ref, rcat_ref, latbuf_ref, ropebuf_ref, sem_ref):
    b = pl.program_id(0)
    c = pl.program_id(1)
    nb = pl.num_programs(0)
    nc = pl.num_programs(1)
    step = b * nc + c
    slot = step % 2

    def page_copies(bb, cc, sl):
        copies = []
        for p in range(PAGES_PER_STEP):
            page = pt_ref[bb, cc * PAGES_PER_STEP + p]
            copies.append(pltpu.make_async_copy(lat_hbm.at[0, page], latbuf_ref.at[sl, p], sem_ref.at[sl, 0]))
            copies.append(pltpu.make_async_copy(rope_hbm.at[0, page], ropebuf_ref.at[sl, p], sem_ref.at[sl, 1]))
        return copies

    @pl.when(step == 0)
    def _():
        for cp in page_copies(b, c, slot):
            cp.start()

    @pl.when(step + 1 < nb * nc)
    def _():
        wrap = c + 1 == nc
        for cp in page_copies(jnp.where(wrap, b + 1, b), jnp.where(wrap, 0, c + 1), 1 - slot):
            cp.start()

    for cp in page_copies(b, c, slot):
        cp.wait()

    @pl.when(c == 0)
    def _():
        m_ref[...] = jnp.full(m_ref.shape, NEG_BIG, F32)
        l_ref[...] = jnp.zeros(l_ref.shape, F32)
        acc_ref[...] = jnp.zeros(acc_ref.shape, F32)

    ql = ql_ref[...]
    qr = qr_ref[:, :D_ROPE]
    for p in range(PAGES_PER_STEP):
        kcat_ref[p * PAGE_SIZE:(p + 1) * PAGE_SIZE, :] = latbuf_ref[slot, p].astype(BF16)
        rcat_ref[:, p * PAGE_SIZE:(p + 1) * PAGE_SIZE] = ropebuf_ref[slot, p].astype(BF16)
    kcat = kcat_ref[...]
    s = _dot_nt(ql, kcat) + _dot(qr, rcat_ref[...])
    m_prev = m_ref[...]
    m_new = jnp.maximum(m_prev, jnp.max(s, -1, keepdims=True))
    corr = jnp.exp(m_prev - m_new)
    pp = jnp.exp(s - m_new)
    m_ref[...] = m_new
    l_ref[...] = corr * l_ref[...] + jnp.sum(pp, -1, keepdims=True)
    acc_ref[...] = corr * acc_ref[...] + _dot(pp.astype(BF16), kcat)

    @pl.when(c == nc - 1)
    def _():
        qlf = ql.astype(F32)
        qrf = qr.astype(F32)
        cn = cn_ref[...].astype(BF16).astype(F32)
        rn = rn_ref[...].astype(BF16).astype(F32)
        t_of_row = lax.broadcasted_iota(jnp.int32, (ROWS, 1), 0) // N_HEADS
        n_new = cn.shape[0]
        s_new = []
        for j in range(n_new):
            sj = (jnp.sum(qlf * cn[j:j + 1, :], -1, keepdims=True)
                  + jnp.sum(qrf * rn[j:j + 1, :], -1, keepdims=True))
            s_new.append(jnp.where(j <= t_of_row, sj, -jnp.inf))
        m0 = m_ref[...]
        m1 = m0
        for sj in s_new:
            m1 = jnp.maximum(m1, sj)
        corr1 = jnp.exp(m0 - m1)
        l1 = corr1 * l_ref[...]
        a1 = corr1 * acc_ref[...]
        for j in range(n_new):
            pj = jnp.exp(s_new[j] - m1)
            l1 = l1 + pj
            a1 = a1 + pj.astype(BF16).astype(F32) * cn[j:j + 1, :]
        o_ref[...] = a1 / l1


def _decode(qlat, qr, ckv_new, kr_new, cache_lat, cache_kr, page_table):
    nb, n_pages = page_table.shape
    nc = n_pages // PAGES_PER_STEP
    n_new = ckv_new.shape[1]

    per_b = lambda rows, width: pl.BlockSpec((None, rows, width), lambda b, c, pt: (b, 0, 0))
    hbm = pl.BlockSpec(memory_space=pl.ANY)
    in_specs = [per_b(ROWS, KV_LORA), per_b(ROWS, LANES), per_b(n_new, KV_LORA), per_b(n_new, D_ROPE), hbm, hbm]
    keys = PAGES_PER_STEP * PAGE_SIZE
    grid_spec = pltpu.PrefetchScalarGridSpec(
        num_scalar_prefetch=1, grid=(nb, nc), in_specs=in_specs,
        out_specs=per_b(ROWS, KV_LORA),
        scratch_shapes=[pltpu.VMEM((ROWS, 1), F32), pltpu.VMEM((ROWS, 1), F32), pltpu.VMEM((ROWS, KV_LORA), F32),
                        pltpu.VMEM((keys, KV_LORA), BF16), pltpu.VMEM((D_ROPE, keys), BF16),
                        pltpu.VMEM((2, PAGES_PER_STEP, PAGE_SIZE, KV_LORA), F32),
                        pltpu.VMEM((2, PAGES_PER_STEP, D_ROPE, PAGE_SIZE), F32),
                        pltpu.SemaphoreType.DMA((2, 2))])
    return pl.pallas_call(
        _decode_kernel, grid_spec=grid_spec,
        out_shape=jax.ShapeDtypeStruct((nb, ROWS, KV_LORA), F32),
        compiler_params=_cparams(("arbitrary", "arbitrary")), name="mla_decode",
    )(page_table, qlat, qr, ckv_new, kr_new, cache_lat, cache_kr)


def _sample_out_kernel(ol_ref, x_ref, wuv_ref, wo_ref, g_ref, b_ref, o_ref):
    olb = ol_ref[...].astype(BF16)
    o = jnp.concatenate(
        [_dot(olb[:, h * KV_LORA:(h + 1) * KV_LORA], wuv_ref[:, h * D_V:(h + 1) * D_V]) for h in range(N_HEADS)],
        axis=1)
    z = ALPHA * x_ref[...] + _dot(o.astype(BF16), wo_ref[...])
    o_ref[...] = _layer_norm(z, g_ref[...], b_ref[...])


def _sample_out(o_lat, x, w_uv, w_o, g, b):
    n = x.shape[0]
    args = (o_lat, x, w_uv, w_o, g, b)
    return pl.pallas_call(
        _sample_out_kernel, grid=(1,), in_specs=[_const_spec(a.shape) for a in args],
        out_specs=_const_spec((n, D_MODEL)), out_shape=jax.ShapeDtypeStruct((n, D_MODEL), F32),
        compiler_params=_cparams(("arbitrary",)), name="mla_sample_out",
    )(*args)


def _sort16_network():
    def merge(lo, hi, r):
        step = r * 2
        if step < hi - lo:
            yield from merge(lo, hi, step)
            yield from merge(lo + r, hi, step)
            yield from [(i, i + r) for i in range(lo + r, hi - r, step)]
        else:
            yield (lo, lo + r)

    def sort(lo, hi):
        if hi - lo >= 1:
            mid = lo + (hi - lo) // 2
            yield from sort(lo, mid)
            yield from sort(mid + 1, hi)
            yield from merge(lo, hi, 1)

    return tuple(sort(0, PEER_TOPK - 1))


_SORT16 = _sort16_network()


def _sort16_desc(v):
    v = list(v)
    for i, j in _SORT16:
        v[i], v[j] = jnp.maximum(v[i], v[j]), jnp.minimum(v[i], v[j])
    return v


def _bitonic_merge16_desc(v):
    v = list(v)
    d = PEER_TOPK // 2
    while d >= 1:
        for i in range(PEER_TOPK):
            if i & d == 0:
                v[i], v[i + d] = jnp.maximum(v[i], v[i + d]), jnp.minimum(v[i], v[i + d])
        d //= 2
    return v


def _merge_top16(x, y):
    return _bitonic_merge16_desc([jnp.maximum(x[k], y[PEER_TOPK - 1 - k]) for k in range(PEER_TOPK)])


def _top16_rows(s):
    v = _sort16_desc([s[SUBLANES * k:SUBLANES * (k + 1), :] for k in range(N_KEYS // SUBLANES)])
    for shift in (4, 2, 1):
        v = _merge_top16(v, [pltpu.roll(a, shift, 0) for a in v])
    return v


def _pair_region():
    return [(a, b) for a in range(PEER_TOPK) for b in range(PEER_TOPK) if (a + 1) * (b + 1) <= PEER_TOPK]


def _peer_route_kernel(x_ref, wqt_ref, keys_ref, s1_ref, e1_ref, s2_ref, e2_ref, tau_ref):
    tn = x_ref.shape[0]
    xb = x_ref[...].astype(BF16)
    qt = _dot_nt(wqt_ref[...], xb)
    sub = lax.broadcasted_iota(jnp.int32, (SUBLANES, tn), 0)
    tops = []
    packed = [[jnp.zeros((SUBLANES, tn), F32) for _ in range(PEER_TOPK)] for _ in range(2)]
    scores = []
    for h in range(PEER_HEADS):
        for half in range(2):
            c = 2 * h + half
            st = _dot(keys_ref[c], qt[c * HALF_KEY:(c + 1) * HALF_KEY, :].astype(BF16))
            scores.append(st)
            top = _top16_rows(st)
            tops.append(top[0])
            for k in range(PEER_TOPK):
                packed[half][k] = jnp.where(sub == h, top[k], packed[half][k])
    cands = [packed[0][a] + packed[1][b] for a, b in _pair_region()]
    pad = (-len(cands)) % PEER_TOPK
    cands += [jnp.full((SUBLANES, tn), -jnp.inf, F32)] * pad
    best = _sort16_desc(cands[:PEER_TOPK])
    for g in range(1, len(cands) // PEER_TOPK):
        best = _merge_top16(best, _sort16_desc(cands[g * PEER_TOPK:(g + 1) * PEER_TOPK]))
    tau = best[PEER_TOPK - 1]
    z = jnp.zeros((SUBLANES, tn), F32)
    for k in range(PEER_TOPK):
        z = z + jnp.exp(best[k] - best[0])
    inv_z = 1.0 / z
    tau_ref[...] = tau
    for h in range(PEER_HEADS):
        m1 = jnp.concatenate([tops[2 * h]] * (N_KEYS // SUBLANES), axis=0)
        m2 = jnp.concatenate([tops[2 * h + 1]] * (N_KEYS // SUBLANES), axis=0)
        s1_ref[h] = scores[2 * h]
        s2_ref[h] = scores[2 * h + 1]
        e1_ref[h] = jnp.exp(scores[2 * h] - m1) * inv_z[h:h + 1, :]
        e2_ref[h] = jnp.exp(scores[2 * h + 1] - m2)


def _peer_route(x, w_qt, keys, layer, tn):
    n = x.shape[0]
    big = jax.ShapeDtypeStruct((PEER_HEADS, N_KEYS, n), F32)
    bspec = pl.BlockSpec((PEER_HEADS, N_KEYS, tn), lambda i: (0, 0, i))
    return pl.pallas_call(
        _peer_route_kernel, grid=(n // tn,),
        in_specs=[pl.BlockSpec((tn, D_MODEL), lambda i: (i, 0)),
                  pl.BlockSpec((None,) + w_qt.shape[1:], lambda i: (layer, 0, 0)),
                  pl.BlockSpec((None,) + keys.shape[1:], lambda i: (layer, 0, 0, 0))],
        out_specs=[bspec, bspec, bspec, bspec, pl.BlockSpec((PEER_HEADS, tn), lambda i: (0, i))],
        out_shape=[big, big, big, big, jax.ShapeDtypeStruct((PEER_HEADS, n), F32)],
        compiler_params=_cparams(("arbitrary",)), name="peer_route",
    )(x, w_qt, keys)


PEER_TE = SUBLANES * N_KEYS
PEER_JB = 2


def _peer_dense_kernel(x_ref, u_ref, vt_ref, s1_ref, e1_ref, s2_ref, e2_ref, tau_ref, g_ref, b_ref,
                       o_ref, acc_ref, h_ref, gate_ref, xb_ref, s1b_ref, e1b_ref, taub_ref):
    j = pl.program_id(1)
    last = pl.num_programs(1) - 1
    tm = x_ref.shape[0]

    def weighted_prev():
        return (_gelu_tanh(h_ref[...]) * gate_ref[...]).astype(BF16)

    @pl.when(j == 0)
    def _():
        acc_ref[...] = jnp.zeros(acc_ref.shape, F32)
        h_ref[...] = jnp.zeros(h_ref.shape, F32)
        gate_ref[...] = jnp.zeros(gate_ref.shape, F32)
        xb_ref[...] = x_ref[...].astype(BF16)
        for h in range(PEER_HEADS):
            taub_ref[h * SUBLANES:(h + 1) * SUBLANES, :] = jnp.broadcast_to(tau_ref[h:h + 1, :], (SUBLANES, tm))

    @pl.when(j < last)
    def _():
        acc_ref[...] += _dot(vt_ref[...], weighted_prev())
        h_ref[...] = _dot_nt(u_ref[...], xb_ref[...])
        for h in range(PEER_HEADS):
            for ii in range(SUBLANES):
                brow = slice((h * SUBLANES + ii) * SUBLANES, (h * SUBLANES + ii + 1) * SUBLANES)
                s1b_ref[brow, :] = jnp.broadcast_to(s1_ref[h, ii:ii + 1, :], (SUBLANES, tm))
                e1b_ref[brow, :] = jnp.broadcast_to(e1_ref[h, ii:ii + 1, :], (SUBLANES, tm))

        def chunk(k, carry):
            j0 = k * (PEER_JB * SUBLANES)
            for lc in range(tm // LANES):
                ls = slice(lc * LANES, (lc + 1) * LANES)
                gates = [[jnp.zeros((SUBLANES, LANES), F32) for _ in range(SUBLANES)] for _ in range(PEER_JB)]
                for h in range(PEER_HEADS):
                    tau = taub_ref[h * SUBLANES:(h + 1) * SUBLANES, ls]
                    jrows = [pl.ds(pl.multiple_of(j0 + t * SUBLANES, SUBLANES), SUBLANES) for t in range(PEER_JB)]
                    s2 = [s2_ref[h, jrows[t], ls] for t in range(PEER_JB)]
                    e2 = [e2_ref[h, jrows[t], ls] for t in range(PEER_JB)]
                    for ii in range(SUBLANES):
                        brow = slice((h * SUBLANES + ii) * SUBLANES, (h * SUBLANES + ii + 1) * SUBLANES)
                        s1 = s1b_ref[brow, ls]
                        e1 = e1b_ref[brow, ls]
                        for t in range(PEER_JB):
                            gates[t][ii] = gates[t][ii] + jnp.where(s2[t] + s1 >= tau, e2[t] * e1, 0.0)
                for t in range(PEER_JB):
                    for ii in range(SUBLANES):
                        r0 = pl.multiple_of(ii * N_KEYS + j0 + t * SUBLANES, SUBLANES)
                        gate_ref[pl.ds(r0, SUBLANES), ls] = gates[t][ii]
            return carry

        lax.fori_loop(0, N_KEYS // (PEER_JB * SUBLANES), chunk, 0)

    @pl.when(j == last)
    def _():
        acc = acc_ref[...] + _dot(vt_ref[...], weighted_prev())
        z = ALPHA * x_ref[...] + acc.T
        o_ref[...] = _layer_norm(z, g_ref[...], b_ref[...])


def _peer_dense(x, u, vt, s1, e1, s2, e2, tau, g, b, layer, tm):
    n = x.shape[0]
    nt = N_EXPERTS // PEER_TE
    full = pl.BlockSpec((PEER_HEADS, N_KEYS, tm), lambda i, j: (0, 0, i))
    part = pl.BlockSpec((PEER_HEADS, SUBLANES, tm), lambda i, j: (0, jnp.minimum(j, nt - 1), i))
    nb = PEER_HEADS * SUBLANES * SUBLANES
    return pl.pallas_call(
        _peer_dense_kernel, grid=(n // tm, nt + 1),
        in_specs=[pl.BlockSpec((tm, D_MODEL), lambda i, j: (i, 0)),
                  pl.BlockSpec((None, PEER_TE, D_MODEL), lambda i, j: (layer, jnp.minimum(j, nt - 1), 0)),
                  pl.BlockSpec((None, D_MODEL, PEER_TE), lambda i, j: (layer, 0, jnp.maximum(j - 1, 0))),
                  part, part, full, full,
                  pl.BlockSpec((PEER_HEADS, tm), lambda i, j: (0, i)),
                  _const_spec(g.shape), _const_spec(b.shape)],
        out_specs=pl.BlockSpec((tm, D_MODEL), lambda i, j: (i, 0)),
        out_shape=jax.ShapeDtypeStruct((n, D_MODEL), F32),
        scratch_shapes=[pltpu.VMEM((D_MODEL, tm), F32), pltpu.VMEM((PEER_TE, tm), F32),
                        pltpu.VMEM((PEER_TE, tm), F32), pltpu.VMEM((tm, D_MODEL), BF16),
                        pltpu.VMEM((nb, tm), F32), pltpu.VMEM((nb, tm), F32),
                        pltpu.VMEM((PEER_HEADS * SUBLANES, tm), F32)],
        compiler_params=_cparams(("arbitrary", "arbitrary")), name="peer_dense",
    )(x, u, vt, s1, e1, s2, e2, tau, g, b)


def _peer_weights(w_q, sub_keys, exp_u, exp_v):
    nl = w_q.shape[0]
    return dict(w_qt=w_q.transpose(0, 2, 1).astype(BF16),
                keys=sub_keys.reshape(nl, 2 * PEER_HEADS, N_KEYS, HALF_KEY).astype(BF16),
                u=exp_u.astype(BF16), vt=exp_v.transpose(0, 2, 1).astype(BF16))


def _peer_ln(x, w, layer, g, b, tn_route, tm):
    s1, e1, s2, e2, tau = _peer_route(x, w["w_qt"], w["keys"], layer, tn_route)
    return _peer_dense(x, w["u"], w["vt"], s1, e1, s2, e2, tau, g, b, layer, tm)


def _s5_disc_kernel(are_ref, aim_ref, ldt_ref, bre_ref, bim_ref, abre_ref, abim_ref, bbre_ref, bbim_ref):
    lr = jnp.minimum(are_ref[...], -1e-4)
    li = aim_ref[...]
    dt = jnp.exp(ldt_ref[...])
    mag = jnp.exp(lr * dt)
    ab_re = mag * jnp.cos(li * dt)
    ab_im = mag * jnp.sin(li * dt)
    den = lr * lr + li * li
    nr, ni = ab_re - 1.0, ab_im
    f_re = (nr * lr + ni * li) / den
    f_im = (ni * lr - nr * li) / den
    br, bi = bre_ref[...], bim_ref[...]
    abre_ref[...] = ab_re
    abim_ref[...] = ab_im
    bbre_ref[...] = f_re * br - f_im * bi
    bbim_ref[...] = f_re * bi + f_im * br


def _s5_weights(w_in, a_re, a_im, log_dt, b_re, b_im, c_re, c_im, d_skip, w_glu, w_o):
    rep = lambda a: jnp.repeat(a, GROUP_SIZE, axis=0)
    bt = lambda b: b.transpose(0, 2, 1).reshape(D_MODEL, STATE)
    args = (rep(a_re), rep(a_im), rep(log_dt.reshape(N_GROUPS, 1)), bt(b_re), bt(b_im))
    sd = jax.ShapeDtypeStruct((D_MODEL, STATE), F32)
    ab_re, ab_im, bb_re, bb_im = pl.pallas_call(
        _s5_disc_kernel, grid=(1,), in_specs=[_const_spec(a.shape) for a in args],
        out_specs=[_const_spec(sd.shape)] * 4, out_shape=[sd] * 4,
        compiler_params=_cparams(("arbitrary",)), name="s5_discretize")(*args)
    nblk = 4
    gl = N_GROUPS // nblk
    eye = jnp.eye(gl, dtype=F32)

    def bdiag_in(bb):
        t = bb.reshape(nblk, gl, GROUP_SIZE, STATE)
        return jnp.einsum('kgpn,gh->kgphn', t, eye).reshape(nblk, gl * GROUP_SIZE, gl * STATE).astype(BF16)

    def bdiag_out(c):
        t = c.reshape(nblk, gl, GROUP_SIZE, STATE)
        return jnp.einsum('kgpn,gh->kgnhp', t, eye).reshape(nblk, gl * STATE, gl * GROUP_SIZE)

    c_cat = jnp.concatenate([bdiag_out(c_re), -bdiag_out(c_im)], axis=1).astype(BF16)
    return dict(w_in=w_in.astype(BF16), b_re=bdiag_in(bb_re), b_im=bdiag_in(bb_im),
                a_re=ab_re[::GROUP_SIZE].reshape(1, N_STATE), a_im=ab_im[::GROUP_SIZE].reshape(1, N_STATE),
                c_cat=c_cat, d=d_skip.reshape(1, D_MODEL), w_glu=w_glu.astype(BF16), w_o=w_o.astype(BF16))


S5_LW = 2 * LANES
S5_NBLK = 4


def _cmul(ar, ai, br, bi):
    return ar * br - ai * bi, ar * bi + ai * br


def _s5_scan_kernel(seg, x_ref, win_ref, bre_ref, bim_ref, are_ref, aim_ref, ccat_ref, d_ref, *rest):
    if seg:
        s0re_ref, s0im_ref, g_ref, ore_ref, oim_ref, sre_ref, sim_ref, tab_ref = rest
    else:
        g_ref, ore_ref, oim_ref, sre_ref, sim_ref, tab_ref, car_ref = rest
    i = pl.program_id(0)
    tl = x_ref.shape[0]
    period = seg if seg else SUBLANES

    @pl.when(i == 0)
    def _():
        ar = jnp.broadcast_to(are_ref[...], (SUBLANES, N_STATE))
        ai = jnp.broadcast_to(aim_ref[...], (SUBLANES, N_STATE))
        rr = lax.broadcasted_iota(jnp.int32, (SUBLANES, N_STATE), 0) % period
        pr, pi = ar, ai
        powers = [(pr, pi)]
        for _ in range(SUBLANES - 1):
            pr, pi = _cmul(pr, pi, ar, ai)
            powers.append((pr, pi))
        for k, d in enumerate((1, 2, 4)):
            tab_ref[2 * k] = jnp.where(rr >= d, powers[d - 1][0], 0.0)
            tab_ref[2 * k + 1] = jnp.where(rr >= d, powers[d - 1][1], 0.0)
        cr = jnp.zeros((SUBLANES, N_STATE), F32)
        ci = jnp.zeros((SUBLANES, N_STATE), F32)
        for p in range(SUBLANES):
            cr = jnp.where(rr == p, powers[p][0], cr)
            ci = jnp.where(rr == p, powers[p][1], ci)
        tab_ref[6] = cr
        tab_ref[7] = ci
        if not seg:
            car_ref[...] = jnp.zeros(car_ref.shape, F32)

    u = _dot(x_ref[...].astype(BF16), win_ref[...])
    ub = u.astype(BF16)
    kin = D_MODEL // S5_NBLK
    kst = N_STATE // S5_NBLK
    for k in range(S5_NBLK):
        sre_ref[:, k * kst:(k + 1) * kst] = _dot(ub[:, k * kin:(k + 1) * kin], bre_ref[k])
        sim_ref[:, k * kst:(k + 1) * kst] = _dot(ub[:, k * kin:(k + 1) * kin], bim_ref[k])

    steps = tuple(d for d in (1, 2, 4) if d < period)

    def body(gi, carry):
        rows = pl.ds(pl.multiple_of(gi * SUBLANES, SUBLANES), SUBLANES)
        for lc in range(N_STATE // S5_LW):
            ls = slice(lc * S5_LW, (lc + 1) * S5_LW)
            xr = sre_ref[rows, ls]
            xi = sim_ref[rows, ls]
            for d in steps:
                k = (1, 2, 4).index(d)
                yr, yi = _cmul(tab_ref[2 * k, :, ls], tab_ref[2 * k + 1, :, ls],
                               pltpu.roll(xr, d, 0), pltpu.roll(xi, d, 0))
                xr, xi = xr + yr, xi + yi
            if seg:
                cr, ci = s0re_ref[rows, ls], s0im_ref[rows, ls]
            else:
                cr, ci = car_ref[0, :, ls], car_ref[1, :, ls]
            yr, yi = _cmul(tab_ref[6, :, ls], tab_ref[7, :, ls], cr, ci)
            xr, xi = xr + yr, xi + yi
            sre_ref[rows, ls] = xr
            sim_ref[rows, ls] = xi
            if not seg:
                car_ref[0, :, ls] = jnp.broadcast_to(xr[SUBLANES - 1:SUBLANES, :], (SUBLANES, S5_LW))
                car_ref[1, :, ls] = jnp.broadcast_to(xi[SUBLANES - 1:SUBLANES, :], (SUBLANES, S5_LW))
        return carry

    lax.fori_loop(0, tl // SUBLANES, body, 0)

    if seg:
        ore_ref[...] = sre_ref[...]
        oim_ref[...] = sim_ref[...]
    else:
        ore_ref[...] = car_ref[0, 0:1, :]
        oim_ref[...] = car_ref[1, 0:1, :]

    kout = D_MODEL // S5_NBLK
    ys = []
    for k in range(S5_NBLK):
        st = jnp.concatenate([sre_ref[:, k * kst:(k + 1) * kst], sim_ref[:, k * kst:(k + 1) * kst]], axis=1)
        ys.append(_dot(st.astype(BF16), ccat_ref[k]))
    y = jnp.concatenate(ys, axis=1) + d_ref[...] * u
    g_ref[...] = _gelu_tanh(y).astype(BF16)


def _s5_scan(x, w, tl, s0=None):
    n = x.shape[0]
    seg = 0 if s0 is None else 4
    row = lambda width: pl.BlockSpec((tl, width), lambda i: (i, 0))
    wargs = (w["w_in"], w["b_re"], w["b_im"], w["a_re"], w["a_im"], w["c_cat"], w["d"])
    in_specs = [row(D_MODEL)] + [_const_spec(a.shape) for a in wargs]
    args = (x,) + wargs
    scratch = [pltpu.VMEM((tl, N_STATE), F32), pltpu.VMEM((tl, N_STATE), F32),
               pltpu.VMEM((8, SUBLANES, N_STATE), F32)]
    if seg:
        in_specs += [row(N_STATE), row(N_STATE)]
        args += tuple(s0)
        st_shape = jax.ShapeDtypeStruct((n, N_STATE), F32)
        st_spec = row(N_STATE)
    else:
        scratch.append(pltpu.VMEM((2, SUBLANES, N_STATE), F32))
        st_shape = jax.ShapeDtypeStruct((1, N_STATE), F32)
        st_spec = _const_spec((1, N_STATE))
    return pl.pallas_call(
        functools.partial(_s5_scan_kernel, seg), grid=(n // tl,), in_specs=in_specs,
        out_specs=[row(D_MODEL), st_spec, st_spec],
        out_shape=[jax.ShapeDtypeStruct((n, D_MODEL), BF16), st_shape, st_shape],
        scratch_shapes=scratch,
        compiler_params=_cparams(("arbitrary",)), name="s5_scan_seg" if seg else "s5_scan",
    )(*args)


def _glu_out_kernel(gin_ref, x_ref, wglu_ref, wo_ref, g_ref, b_ref, o_ref):
    ga = _dot(gin_ref[...], wglu_ref[...])
    hid = ga[:, :D_MODEL] * jax.nn.sigmoid(ga[:, D_MODEL:])
    z = ALPHA * x_ref[...] + _dot(hid.astype(BF16), wo_ref[...])
    o_ref[...] = _layer_norm(z, g_ref[...], b_ref[...])


def _glu_out(gin, x, w_glu, w_o, g, b, tn):
    n = x.shape[0]
    row = lambda: pl.BlockSpec((tn, D_MODEL), lambda i: (i, 0))
    return pl.pallas_call(
        _glu_out_kernel, grid=(n // tn,),
        in_specs=[row(), row(), _const_spec(w_glu.shape), _const_spec(w_o.shape),
                  _const_spec(g.shape), _const_spec(b.shape)],
        out_specs=row(), out_shape=jax.ShapeDtypeStruct((n, D_MODEL), F32),
        compiler_params=_cparams(("arbitrary",)), name="s5_glu_out",
    )(gin, x, w_glu, w_o, g, b)


def _tiles(n):
    fit = lambda pref: pref if n % pref == 0 else n
    return dict(proj=fit(512), flash=fit(512), scan=fit(256), seg=fit(128), glu=fit(512),
                route=fit(256), dense=fit(512))


def kernel(x_prompt, x_sample, cache_kv_latent, cache_k_rope, state_ssm_re, state_ssm_im, page_table, mla_w_in, mla_g_q, mla_g_kv, mla_w_uq, mla_w_uk, mla_w_uv, mla_w_o, ssm_w_in, ssm_a_re, ssm_a_im, ssm_log_dt, ssm_b_re, ssm_b_im, ssm_c_re, ssm_c_im, ssm_d, ssm_w_glu, ssm_w_o, peer_w_q, peer_sub_keys, peer_u, peer_v, ln_g, ln_b):
    bp, lp, _ = x_prompt.shape
    bd, td, _ = x_sample.shape
    assert bp == 1 and td * N_HEADS == ROWS
    npr, nsm = bp * lp, bd * td
    yp = x_prompt.reshape(npr, D_MODEL)
    ys = x_sample.reshape(nsm, D_MODEL)
    lng = lambda layer, k: ln_g[layer, k].reshape(1, D_MODEL)
    lnb = lambda layer, k: ln_b[layer, k].reshape(1, D_MODEL)
    tp, ts = _tiles(npr), _tiles(nsm)
    pw = _peer_weights(peer_w_q, peer_sub_keys, peer_u, peer_v)

    outs = {}
    for layer in range(DEPTH):
        j = layer // 2
        if layer % 2 == 0:
            w = _mla_weights(mla_w_in[j], mla_g_q[j], mla_g_kv[j], mla_w_uq[j], mla_w_uk[j], mla_w_uv[j], mla_w_o[j])
            cs_p = _rope_cs(jnp.arange(lp))
            ckv_p, kr_p, qn, qr, kn, krp, v = _mla_proj(yp, cs_p, w, False, tp["proj"])
            yp = _flash_prompt(qn, qr, kn, krp, v, yp, w["w_o"], lng(layer, 0), lnb(layer, 0), tp["flash"])
            cs_s = _rope_cs(jnp.tile(PAST_LEN + jnp.arange(td), bd))
            ckv_s, kr_s, qlat, qr_s = _mla_proj(ys, cs_s, w, True, nsm)
            o_lat = _decode(qlat.reshape(bd, ROWS, KV_LORA), qr_s.reshape(bd, ROWS, LANES),
                            ckv_s.reshape(bd, td, KV_LORA), kr_s.reshape(bd, td, D_ROPE),
                            cache_kv_latent[j:j + 1], jnp.swapaxes(cache_k_rope[j:j + 1], 2, 3), page_table)
            ys = _sample_out(o_lat.reshape(nsm, N_HEADS * KV_LORA), ys, w["w_uv"], w["w_o"],
                             lng(layer, 0), lnb(layer, 0))
            outs.setdefault("p_lat", []).append(ckv_p.reshape(bp, lp, KV_LORA))
            outs.setdefault("p_kr", []).append(kr_p.reshape(bp, lp, D_ROPE))
            outs.setdefault("s_lat", []).append(ckv_s.reshape(bd, td, KV_LORA))
            outs.setdefault("s_kr", []).append(kr_s.reshape(bd, td, D_ROPE))
        else:
            w = _s5_weights(ssm_w_in[j], ssm_a_re[j], ssm_a_im[j], ssm_log_dt[j], ssm_b_re[j], ssm_b_im[j],
                            ssm_c_re[j], ssm_c_im[j], ssm_d[j], ssm_w_glu[j], ssm_w_o[j])
            g_p, sre_p, sim_p = _s5_scan(yp, w, tp["scan"])
            yp = _glu_out(g_p, yp, w["w_glu"], w["w_o"], lng(layer, 0), lnb(layer, 0), tp["glu"])
            s0 = (jnp.repeat(state_ssm_re[j].reshape(bd, N_STATE), td, axis=0),
                  jnp.repeat(state_ssm_im[j].reshape(bd, N_STATE), td, axis=0))
            g_s, sre_s, sim_s = _s5_scan(ys, w, ts["seg"], s0)
            ys = _glu_out(g_s, ys, w["w_glu"], w["w_o"], lng(layer, 0), lnb(layer, 0), ts["glu"])
            outs.setdefault("p_sre", []).append(sre_p.reshape(bp, N_GROUPS, STATE))
            outs.setdefault("p_sim", []).append(sim_p.reshape(bp, N_GROUPS, STATE))
            outs.setdefault("s_sre", []).append(sre_s[td - 1::td].reshape(bd, N_GROUPS, STATE))
            outs.setdefault("s_sim", []).append(sim_s[td - 1::td].reshape(bd, N_GROUPS, STATE))
        yp = _peer_ln(yp, pw, layer, lng(layer, 1), lnb(layer, 1), tp["route"], tp["dense"])
        ys = _peer_ln(ys, pw, layer, lng(layer, 1), lnb(layer, 1), ts["route"], ts["dense"])

    return (yp.reshape(bp, lp, D_MODEL), ys.reshape(bd, td, D_MODEL),
            jnp.stack(outs["p_lat"]), jnp.stack(outs["p_kr"]), jnp.stack(outs["p_sre"]), jnp.stack(outs["p_sim"]),
            jnp.stack(outs["s_lat"]), jnp.stack(outs["s_kr"]), jnp.stack(outs["s_sre"]), jnp.stack(outs["s_sim"]))
```

```python
import functools
import math

import jax
import jax.numpy as jnp
from jax import lax
from jax.experimental import pallas as pl
from jax.experimental.pallas import tpu as pltpu

F32 = jnp.float32
BF16 = jnp.bfloat16

D_MODEL = 1024
DEPTH = 2
PAST_LEN = 16384
PAGE_SIZE = 128

N_HEADS = 8
D_NOPE = 128
D_ROPE = 64
D_V = 128
Q_LORA = 384
KV_LORA = 256
ROPE_THETA = 10000.0
ATTN_SCALE = 1.0 / math.sqrt(D_NOPE + D_ROPE)

GROUP_SIZE = 16
N_GROUPS = D_MODEL // GROUP_SIZE
STATE = 64
N_STATE = N_GROUPS * STATE

PEER_HEADS = 8
N_KEYS = 128
N_EXPERTS = N_KEYS * N_KEYS
HALF_KEY = 128
PEER_TOPK = 16

ALPHA = (2 * DEPTH) ** 0.25
LN_EPS = 1e-5
RMS_EPS = 1e-6

LANES = 128
SUBLANES = 8
VMEM_LIMIT = 56 * 1024 * 1024

NEG_BIG = -1e30


def _cparams(sem):
    return pltpu.CompilerParams(dimension_semantics=sem, vmem_limit_bytes=VMEM_LIMIT)


def _dot(a, b):
    return jnp.dot(a, b, preferred_element_type=F32)


def _dot_nt(a, b):
    return lax.dot_general(a, b, (((1,), (1,)), ((), ())), preferred_element_type=F32)


def _layer_norm(z, g, b):
    mu = jnp.mean(z, -1, keepdims=True)
    zc = z - mu
    var = jnp.mean(zc * zc, -1, keepdims=True)
    return zc * lax.rsqrt(var + LN_EPS) * g + b


def _rms_norm(x, g):
    return x * lax.rsqrt(jnp.mean(x * x, -1, keepdims=True) + RMS_EPS) * g


def _gelu_tanh(x):
    c = math.sqrt(2.0 / math.pi)
    return x * (0.5 * (1.0 + jnp.tanh(c * (x + 0.044715 * (x * x * x)))))


def _const_spec(shape):
    n = len(shape)
    return pl.BlockSpec(shape, lambda *_: (0,) * n)


def _mla_proj_kernel(absorb, x_ref, wcq_ref, wckv_ref, wkr_ref, gq_ref, gkv_ref,
                     wuqn_ref, wuqr_ref, wuqs_ref, cs_ref, wa_ref, wb_ref, *out_refs):
    xb = x_ref[...].astype(BF16)
    c_q = _rms_norm(_dot(xb, wcq_ref[...]), gq_ref[...])
    c_kv = _rms_norm(_dot(xb, wckv_ref[...]), gkv_ref[...])
    hk = _dot(xb, wkr_ref[...])
    cos = cs_ref[:, :LANES]
    sin = cs_ref[:, LANES:]
    kr = hk[:, :LANES] * cos + hk[:, LANES:] * sin
    cqb = c_q.astype(BF16)
    cos8 = jnp.concatenate([cos] * N_HEADS, axis=1)
    sin8 = jnp.concatenate([sin] * N_HEADS, axis=1)
    qscale = ATTN_SCALE if absorb else ATTN_SCALE * math.log2(math.e)
    qn = _dot(cqb, wuqn_ref[...]) * qscale
    qr = (_dot(cqb, wuqr_ref[...]) * cos8 + _dot(cqb, wuqs_ref[...]) * sin8) * qscale
    ckvb = c_kv.astype(BF16)
    if absorb:
        ckv_ref, kr_ref, qlat_ref, qr_ref = out_refs
        qnb = qn.astype(BF16)
        for h in range(N_HEADS):
            qlat_ref[:, h * KV_LORA:(h + 1) * KV_LORA] = _dot(
                qnb[:, h * D_NOPE:(h + 1) * D_NOPE], wa_ref[h * D_NOPE:(h + 1) * D_NOPE, :]).astype(BF16)
    else:
        ckv_ref, kr_ref, qn_ref, qr_ref, kn_ref, krp_ref, v_ref = out_refs
        qn_ref[...] = qn.astype(BF16)
        kn_ref[...] = _dot(ckvb, wa_ref[...]).astype(BF16)
        v_ref[...] = _dot(ckvb, wb_ref[...]).astype(BF16)
        krp_ref[...] = kr.astype(BF16)
    ckv_ref[...] = c_kv
    kr_ref[...] = kr[:, :D_ROPE]
    qr_ref[...] = qr.astype(BF16)


def _rope_cs(pos):
    inv = 1.0 / (ROPE_THETA ** (jnp.arange(0, D_ROPE, 2, dtype=F32) / D_ROPE))
    ang = pos.astype(F32)[:, None] * inv[None, :]
    cos, sin = jnp.cos(ang), jnp.sin(ang)
    z = jnp.zeros((pos.shape[0], LANES - D_ROPE), F32)
    return jnp.concatenate([cos, cos, z, -sin, sin, z], axis=1)


def _mla_weights(w_in, g_q, g_kv, w_uq, w_uk, w_uv, w_o):
    half = D_ROPE // 2
    w_cq = w_in[:, :Q_LORA].astype(BF16)
    w_ckv = w_in[:, Q_LORA:Q_LORA + KV_LORA].astype(BF16)
    w_k = w_in[:, Q_LORA + KV_LORA:]
    zk = jnp.zeros((D_MODEL, LANES - D_ROPE), F32)
    w_kr = jnp.concatenate([w_k, zk, w_k[:, half:], w_k[:, :half], zk], axis=1).astype(BF16)
    wq = w_uq.reshape(Q_LORA, N_HEADS, D_NOPE + D_ROPE)
    wq_n = wq[:, :, :D_NOPE].reshape(Q_LORA, N_HEADS * D_NOPE).astype(BF16)
    r = wq[:, :, D_NOPE:]
    zq = jnp.zeros((Q_LORA, N_HEADS, LANES - D_ROPE), F32)
    wq_r = jnp.concatenate([r, zq], axis=2).reshape(Q_LORA, N_HEADS * LANES).astype(BF16)
    wq_s = jnp.concatenate([r[:, :, half:], r[:, :, :half], zq], axis=2).reshape(Q_LORA, N_HEADS * LANES).astype(BF16)
    return dict(w_cq=w_cq, w_ckv=w_ckv, w_kr=w_kr, g_q=g_q.reshape(1, -1), g_kv=g_kv.reshape(1, -1),
                wq_n=wq_n, wq_r=wq_r, wq_s=wq_s, w_uk=w_uk.astype(BF16), w_ukT=w_uk.T.astype(BF16),
                w_uv=w_uv.astype(BF16), w_o=w_o.astype(BF16))


def _mla_proj(x, cs, w, absorb, tn):
    n = x.shape[0]
    hw = N_HEADS * LANES
    row = lambda width: pl.BlockSpec((tn, width), lambda i: (i, 0))
    wa = w["w_ukT"] if absorb else w["w_uk"]
    wb = w["w_uv"]
    in_specs = [row(D_MODEL), _const_spec(w["w_cq"].shape), _const_spec(w["w_ckv"].shape),
                _const_spec(w["w_kr"].shape), _const_spec(w["g_q"].shape), _const_spec(w["g_kv"].shape),
                _const_spec(w["wq_n"].shape), _const_spec(w["wq_r"].shape), _const_spec(w["wq_s"].shape),
                row(2 * LANES), _const_spec(wa.shape), _const_spec(wb.shape)]
    if absorb:
        out_shape = [jax.ShapeDtypeStruct((n, KV_LORA), F32), jax.ShapeDtypeStruct((n, D_ROPE), F32),
                     jax.ShapeDtypeStruct((n, N_HEADS * KV_LORA), BF16), jax.ShapeDtypeStruct((n, hw), BF16)]
        out_specs = [row(KV_LORA), row(D_ROPE), row(N_HEADS * KV_LORA), row(hw)]
    else:
        out_shape = [jax.ShapeDtypeStruct((n, KV_LORA), F32), jax.ShapeDtypeStruct((n, D_ROPE), F32),
                     jax.ShapeDtypeStruct((n, hw), BF16), jax.ShapeDtypeStruct((n, hw), BF16),
                     jax.ShapeDtypeStruct((n, hw), BF16), jax.ShapeDtypeStruct((n, LANES), BF16),
                     jax.ShapeDtypeStruct((n, hw), BF16)]
        out_specs = [row(KV_LORA), row(D_ROPE), row(hw), row(hw), row(hw), row(LANES), row(hw)]
    return pl.pallas_call(
        functools.partial(_mla_proj_kernel, absorb),
        grid=(n // tn,), in_specs=in_specs, out_specs=out_specs, out_shape=out_shape,
        compiler_params=_cparams(("arbitrary",)), name="mla_proj_absorb" if absorb else "mla_proj",
    )(x, w["w_cq"], w["w_ckv"], w["w_kr"], w["g_q"], w["g_kv"], w["wq_n"], w["wq_r"], w["wq_s"], cs, wa, wb)


def _flash_kernel(qn_ref, qr_ref, kn_ref, kr_ref, v_ref, x_ref, wo_ref, g_ref, b_ref, o_ref,
                  m_ref, acc_ref):
    qi = pl.program_id(0)
    ki = pl.program_id(1)
    tq = qn_ref.shape[0]
    tk = kn_ref.shape[0]

    @pl.when(ki == 0)
    def _():
        m_ref[...] = jnp.full(m_ref.shape, NEG_BIG, F32)
        acc_ref[...] = jnp.zeros(acc_ref.shape, F32)

    def step(masked):
        kr = kr_ref[...]
        ones = jnp.ones((tk, LANES), BF16)
        if masked:
            keep = (lax.broadcasted_iota(jnp.int32, (tq, tk), 1)
                    <= lax.broadcasted_iota(jnp.int32, (tq, tk), 0))

        def scores(h):
            hs = slice(h * LANES, (h + 1) * LANES)
            q = jnp.concatenate([qn_ref[:, hs], qr_ref[:, hs]], axis=1)
            k = jnp.concatenate([kn_ref[:, hs], kr], axis=1)
            return _dot_nt(q, k)

        s_next = scores(0)
        for h in range(N_HEADS):
            s = s_next
            if h + 1 < N_HEADS:
                s_next = scores(h + 1)
            if masked:
                s = jnp.where(keep, s, -jnp.inf)
            hs = slice(h * LANES, (h + 1) * LANES)
            m_prev = m_ref[h]
            m_new = jnp.maximum(m_prev, jnp.max(s, -1, keepdims=True))
            p = jnp.exp2(s - jnp.concatenate([m_new] * (tk // LANES), axis=1))
            corr = jnp.exp2(m_prev - m_new)
            v1 = jnp.concatenate([v_ref[:, hs], ones], axis=1)
            acc_ref[h] = jnp.concatenate([corr, corr], axis=1) * acc_ref[h] + _dot(p.astype(BF16), v1)
            m_ref[h] = m_new

    @pl.when(ki < qi)
    def _():
        step(False)

    @pl.when(ki == qi)
    def _():
        step(True)
        o = jnp.concatenate([acc_ref[h, :, :D_V] / acc_ref[h, :, D_V:] for h in range(N_HEADS)], axis=1)
        z = ALPHA * x_ref[...] + _dot(o.astype(BF16), wo_ref[...])
        o_ref[...] = _layer_norm(z, g_ref[...], b_ref[...])


def _flash_prompt(qn, qr, kn, krp, v, x, w_o, g, b, tq):
    n = x.shape[0]
    hw = N_HEADS * LANES
    qspec = lambda width: pl.BlockSpec((tq, width), lambda i, j: (i, 0))
    kspec = lambda width: pl.BlockSpec((tq, width), lambda i, j: (jnp.minimum(i, j), 0))
    return pl.pallas_call(
        _flash_kernel,
        grid=(n // tq, n // tq),
        in_specs=[qspec(hw), qspec(hw), kspec(hw), kspec(LANES), kspec(hw), qspec(D_MODEL),
                  _const_spec(w_o.shape), _const_spec(g.shape), _const_spec(b.shape)],
        out_specs=qspec(D_MODEL),
        out_shape=jax.ShapeDtypeStruct((n, D_MODEL), F32),
        scratch_shapes=[pltpu.VMEM((N_HEADS, tq, LANES), F32), pltpu.VMEM((N_HEADS, tq, D_V + LANES), F32)],
        compiler_params=_cparams(("arbitrary", "arbitrary")), name="mla_flash_prompt",
    )(qn, qr, kn, krp, v, x, w_o, g, b)


PAGES_PER_STEP = 32
ROWS = 32


def _decode_kernel(pt_ref, ql_ref, qr_ref, cn_ref, rn_ref, lat_hbm, rope_hbm, o_ref,
                   m_ref, l_ref, acc_ref, kcat_ref, rcat_ref, latbuf_ref, ropebuf_ref, sem_ref):
    b = pl.program_id(0)
    c = pl.program_id(1)
    nb = pl.num_programs(0)
    nc = pl.num_programs(1)
    step = b * nc + c
    slot = step % 2

    def page_copies(bb, cc, sl):
        copies = []
        for p in range(PAGES_PER_STEP):
            page = pt_ref[bb, cc * PAGES_PER_STEP + p]
            copies.append(pltpu.make_async_copy(lat_hbm.at[0, page], latbuf_ref.at[sl, p], sem_ref.at[sl, 0]))
            copies.append(pltpu.make_async_copy(rope_hbm.at[0, page], ropebuf_ref.at[sl, p], sem_ref.at[sl, 1]))
        return copies

    def start_all(copies):
        for n, cp in enumerate(copies):
            cp.start(priority=(n // 2) % 2)

    @pl.when(step == 0)
    def _():
        start_all(page_copies(b, c, slot))

    @pl.when(step + 1 < nb * nc)
    def _():
        wrap = c + 1 == nc
        start_all(page_copies(jnp.where(wrap, b + 1, b), jnp.where(wrap, 0, c + 1), 1 - slot))

    for cp in page_copies(b, c, slot):
        cp.wait()

    @pl.when(c == 0)
    def _():
        m_ref[...] = jnp.full(m_ref.shape, NEG_BIG, F32)
        l_ref[...] = jnp.zeros(l_ref.shape, F32)
        acc_ref[...] = jnp.zeros(acc_ref.shape, F32)

    ql = ql_ref[...]
    qr = qr_ref[:, :D_ROPE]
    for p in range(PAGES_PER_STEP):
        kcat_ref[p * PAGE_SIZE:(p + 1) * PAGE_SIZE, :] = latbuf_ref[slot, p].astype(BF16)
        rcat_ref[:, p * PAGE_SIZE:(p + 1) * PAGE_SIZE] = ropebuf_ref[slot, p].astype(BF16)
    kcat = kcat_ref[...]
    s = _dot_nt(ql, kcat) + _dot(qr, rcat_ref[...])
    m_prev = m_ref[...]
    m_new = jnp.maximum(m_prev, jnp.max(s, -1, keepdims=True))
    corr = jnp.exp(m_prev - m_new)
    pp = jnp.exp(s - m_new)
    m_ref[...] = m_new
    l_ref[...] = corr * l_ref[...] + jnp.sum(pp, -1, keepdims=True)
    acc_ref[...] = corr * acc_ref[...] + _dot(pp.astype(BF16), kcat)

    @pl.when(c == nc - 1)
    def _():
        qlf = ql.astype(F32)
        qrf = qr.astype(F32)
        cn = cn_ref[...].astype(BF16).astype(F32)
        rn = rn_ref[...].astype(BF16).astype(F32)
        t_of_row = lax.broadcasted_iota(jnp.int32, (ROWS, 1), 0) // N_HEADS
        n_new = cn.shape[0]
        s_new = []
        for j in range(n_new):
            sj = (jnp.sum(qlf * cn[j:j + 1, :], -1, keepdims=True)
                  + jnp.sum(qrf * rn[j:j + 1, :], -1, keepdims=True))
            s_new.append(jnp.where(j <= t_of_row, sj, -jnp.inf))
        m0 = m_ref[...]
        m1 = m0
        for sj in s_new:
            m1 = jnp.maximum(m1, sj)
        corr1 = jnp.exp(m0 - m1)
        l1 = corr1 * l_ref[...]
        a1 = corr1 * acc_ref[...]
        for j in range(n_new):
            pj = jnp.exp(s_new[j] - m1)
            l1 = l1 + pj
            a1 = a1 + pj.astype(BF16).astype(F32) * cn[j:j + 1, :]
        o_ref[...] = a1 / l1


def _decode(qlat, qr, ckv_new, kr_new, cache_lat, cache_kr, page_table):
    nb, n_pages = page_table.shape
    nc = n_pages // PAGES_PER_STEP
    n_new = ckv_new.shape[1]

    per_b = lambda rows, width: pl.BlockSpec((None, rows, width), lambda b, c, pt: (b, 0, 0))
    hbm = pl.BlockSpec(memory_space=pl.ANY)
    in_specs = [per_b(ROWS, KV_LORA), per_b(ROWS, LANES), per_b(n_new, KV_LORA), per_b(n_new, D_ROPE), hbm, hbm]
    keys = PAGES_PER_STEP * PAGE_SIZE
    grid_spec = pltpu.PrefetchScalarGridSpec(
        num_scalar_prefetch=1, grid=(nb, nc), in_specs=in_specs,
        out_specs=per_b(ROWS, KV_LORA),
        scratch_shapes=[pltpu.VMEM((ROWS, 1), F32), pltpu.VMEM((ROWS, 1), F32), pltpu.VMEM((ROWS, KV_LORA), F32),
                        pltpu.VMEM((keys, KV_LORA), BF16), pltpu.VMEM((D_ROPE, keys), BF16),
                        pltpu.VMEM((2, PAGES_PER_STEP, PAGE_SIZE, KV_LORA), F32),
                        pltpu.VMEM((2, PAGES_PER_STEP, D_ROPE, PAGE_SIZE), F32),
                        pltpu.SemaphoreType.DMA((2, 2))])
    return pl.pallas_call(
        _decode_kernel, grid_spec=grid_spec,
        out_shape=jax.ShapeDtypeStruct((nb, ROWS, KV_LORA), F32),
        compiler_params=_cparams(("arbitrary", "arbitrary")), name="mla_decode",
    )(page_table, qlat, qr, ckv_new, kr_new, cache_lat, cache_kr)


def _sample_out_kernel(ol_ref, x_ref, wuv_ref, wo_ref, g_ref, b_ref, o_ref):
    olb = ol_ref[...].astype(BF16)
    o = jnp.concatenate(
        [_dot(olb[:, h * KV_LORA:(h + 1) * KV_LORA], wuv_ref[:, h * D_V:(h + 1) * D_V]) for h in range(N_HEADS)],
        axis=1)
    z = ALPHA * x_ref[...] + _dot(o.astype(BF16), wo_ref[...])
    o_ref[...] = _layer_norm(z, g_ref[...], b_ref[...])


def _sample_out(o_lat, x, w_uv, w_o, g, b):
    n = x.shape[0]
    args = (o_lat, x, w_uv, w_o, g, b)
    return pl.pallas_call(
        _sample_out_kernel, grid=(1,), in_specs=[_const_spec(a.shape) for a in args],
        out_specs=_const_spec((n, D_MODEL)), out_shape=jax.ShapeDtypeStruct((n, D_MODEL), F32),
        compiler_params=_cparams(("arbitrary",)), name="mla_sample_out",
    )(*args)


def _sort16_network():
    def merge(lo, hi, r):
        step = r * 2
        if step < hi - lo:
            yield from merge(lo, hi, step)
            yield from merge(lo + r, hi, step)
            yield from [(i, i + r) for i in range(lo + r, hi - r, step)]
        else:
            yield (lo, lo + r)

    def sort(lo, hi):
        if hi - lo >= 1:
            mid = lo + (hi - lo) // 2
            yield from sort(lo, mid)
            yield from sort(mid + 1, hi)
            yield from merge(lo, hi, 1)

    return tuple(sort(0, PEER_TOPK - 1))


_SORT16 = _sort16_network()


def _sort16_desc(v):
    v = list(v)
    for i, j in _SORT16:
        v[i], v[j] = jnp.maximum(v[i], v[j]), jnp.minimum(v[i], v[j])
    return v


def _bitonic_merge16_desc(v):
    v = list(v)
    d = PEER_TOPK // 2
    while d >= 1:
        for i in range(PEER_TOPK):
            if i & d == 0:
                v[i], v[i + d] = jnp.maximum(v[i], v[i + d]), jnp.minimum(v[i], v[i + d])
        d //= 2
    return v


def _merge_top16(x, y):
    return _bitonic_merge16_desc([jnp.maximum(x[k], y[PEER_TOPK - 1 - k]) for k in range(PEER_TOPK)])


def _top16_rows(s):
    v = _sort16_desc([s[SUBLANES * k:SUBLANES * (k + 1), :] for k in range(N_KEYS // SUBLANES)])
    for shift in (4, 2, 1):
        v = _merge_top16(v, [pltpu.roll(a, shift, 0) for a in v])
    return v


def _pair_region():
    return [(a, b) for a in range(PEER_TOPK) for b in range(PEER_TOPK) if (a + 1) * (b + 1) <= PEER_TOPK]


def _peer_route_kernel(x_ref, wqt_ref, keys_ref, s1_ref, e1_ref, s2_ref, e2_ref, tau_ref):
    tn = x_ref.shape[0]
    xb = x_ref[...].astype(BF16)
    qt = _dot_nt(wqt_ref[...], xb)
    sub = lax.broadcasted_iota(jnp.int32, (SUBLANES, tn), 0)
    tops = []
    packed = [[jnp.zeros((SUBLANES, tn), F32) for _ in range(PEER_TOPK)] for _ in range(2)]
    scores = []
    for h in range(PEER_HEADS):
        for half in range(2):
            c = 2 * h + half
            st = _dot(keys_ref[c], qt[c * HALF_KEY:(c + 1) * HALF_KEY, :].astype(BF16))
            scores.append(st)
            top = _top16_rows(st)
            tops.append(top[0])
            for k in range(PEER_TOPK):
                packed[half][k] = jnp.where(sub == h, top[k], packed[half][k])
    cands = [packed[0][a] + packed[1][b] for a, b in _pair_region()]
    pad = (-len(cands)) % PEER_TOPK
    cands += [jnp.full((SUBLANES, tn), -jnp.inf, F32)] * pad
    best = _sort16_desc(cands[:PEER_TOPK])
    for g in range(1, len(cands) // PEER_TOPK):
        best = _merge_top16(best, _sort16_desc(cands[g * PEER_TOPK:(g + 1) * PEER_TOPK]))
    tau = best[PEER_TOPK - 1]
    z = jnp.zeros((SUBLANES, tn), F32)
    for k in range(PEER_TOPK):
        z = z + jnp.exp(best[k] - best[0])
    inv_z = 1.0 / z
    tau_ref[...] = tau
    for h in range(PEER_HEADS):
        m1 = jnp.concatenate([tops[2 * h]] * (N_KEYS // SUBLANES), axis=0)
        m2 = jnp.concatenate([tops[2 * h + 1]] * (N_KEYS // SUBLANES), axis=0)
        s1_ref[h] = scores[2 * h]
        s2_ref[h] = scores[2 * h + 1]
        e1_ref[h] = jnp.exp(scores[2 * h] - m1) * inv_z[h:h + 1, :]
        e2_ref[h] = jnp.exp(scores[2 * h + 1] - m2)


def _peer_route(x, w_qt, keys, layer, tn):
    n = x.shape[0]
    big = jax.ShapeDtypeStruct((PEER_HEADS, N_KEYS, n), F32)
    bspec = pl.BlockSpec((PEER_HEADS, N_KEYS, tn), lambda i: (0, 0, i))
    return pl.pallas_call(
        _peer_route_kernel, grid=(n // tn,),
        in_specs=[pl.BlockSpec((tn, D_MODEL), lambda i: (i, 0)),
                  pl.BlockSpec((None,) + w_qt.shape[1:], lambda i: (layer, 0, 0)),
                  pl.BlockSpec((None,) + keys.shape[1:], lambda i: (layer, 0, 0, 0))],
        out_specs=[bspec, bspec, bspec, bspec, pl.BlockSpec((PEER_HEADS, tn), lambda i: (0, i))],
        out_shape=[big, big, big, big, jax.ShapeDtypeStruct((PEER_HEADS, n), F32)],
        compiler_params=_cparams(("arbitrary",)), name="peer_route",
    )(x, w_qt, keys)


PEER_TE = SUBLANES * N_KEYS
PEER_JB = 2


def _peer_dense_kernel(x_ref, u_ref, vt_ref, s1_ref, e1_ref, s2_ref, e2_ref, tau_ref, g_ref, b_ref,
                       o_ref, acc_ref, h_ref, gate_ref, xb_ref, s1b_ref, e1b_ref, taub_ref):
    j = pl.program_id(1)
    last = pl.num_programs(1) - 1
    tm = x_ref.shape[0]

    def weighted_prev():
        return (_gelu_tanh(h_ref[...]) * gate_ref[...]).astype(BF16)

    @pl.when(j == 0)
    def _():
        acc_ref[...] = jnp.zeros(acc_ref.shape, F32)
        h_ref[...] = jnp.zeros(h_ref.shape, F32)
        gate_ref[...] = jnp.zeros(gate_ref.shape, F32)
        xb_ref[...] = x_ref[...].astype(BF16)
        for h in range(PEER_HEADS):
            taub_ref[h * SUBLANES:(h + 1) * SUBLANES, :] = jnp.broadcast_to(tau_ref[h:h + 1, :], (SUBLANES, tm))

    @pl.when(j < last)
    def _():
        acc_ref[...] += _dot(vt_ref[...], weighted_prev())
        h_ref[...] = _dot_nt(u_ref[...], xb_ref[...])
        for h in range(PEER_HEADS):
            for ii in range(SUBLANES):
                brow = slice((h * SUBLANES + ii) * SUBLANES, (h * SUBLANES + ii + 1) * SUBLANES)
                s1b_ref[brow, :] = jnp.broadcast_to(s1_ref[h, ii:ii + 1, :], (SUBLANES, tm))
                e1b_ref[brow, :] = jnp.broadcast_to(e1_ref[h, ii:ii + 1, :], (SUBLANES, tm))

        def chunk(k, carry):
            j0 = k * (PEER_JB * SUBLANES)
            for lc in range(tm // LANES):
                ls = slice(lc * LANES, (lc + 1) * LANES)
                gates = [[jnp.zeros((SUBLANES, LANES), F32) for _ in range(SUBLANES)] for _ in range(PEER_JB)]
                for h in range(PEER_HEADS):
                    tau = taub_ref[h * SUBLANES:(h + 1) * SUBLANES, ls]
                    jrows = [pl.ds(pl.multiple_of(j0 + t * SUBLANES, SUBLANES), SUBLANES) for t in range(PEER_JB)]
                    s2 = [s2_ref[h, jrows[t], ls] for t in range(PEER_JB)]
                    e2 = [e2_ref[h, jrows[t], ls] for t in range(PEER_JB)]
                    for ii in range(SUBLANES):
                        brow = slice((h * SUBLANES + ii) * SUBLANES, (h * SUBLANES + ii + 1) * SUBLANES)
                        s1 = s1b_ref[brow, ls]
                        e1 = e1b_ref[brow, ls]
                        for t in range(PEER_JB):
                            gates[t][ii] = gates[t][ii] + jnp.where(s2[t] + s1 >= tau, e2[t] * e1, 0.0)
                for t in range(PEER_JB):
                    for ii in range(SUBLANES):
                        r0 = pl.multiple_of(ii * N_KEYS + j0 + t * SUBLANES, SUBLANES)
                        gate_ref[pl.ds(r0, SUBLANES), ls] = gates[t][ii]
            return carry

        lax.fori_loop(0, N_KEYS // (PEER_JB * SUBLANES), chunk, 0)

    @pl.when(j == last)
    def _():
        acc = acc_ref[...] + _dot(vt_ref[...], weighted_prev())
        z = ALPHA * x_ref[...] + acc.T
        o_ref[...] = _layer_norm(z, g_ref[...], b_ref[...])


def _peer_dense(x, u, vt, s1, e1, s2, e2, tau, g, b, layer, tm):
    n = x.shape[0]
    nt = N_EXPERTS // PEER_TE
    full = pl.BlockSpec((PEER_HEADS, N_KEYS, tm), lambda i, j: (0, 0, i))
    part = pl.BlockSpec((PEER_HEADS, SUBLANES, tm), lambda i, j: (0, jnp.minimum(j, nt - 1), i))
    nb = PEER_HEADS * SUBLANES * SUBLANES
    return pl.pallas_call(
        _peer_dense_kernel, grid=(n // tm, nt + 1),
        in_specs=[pl.BlockSpec((tm, D_MODEL), lambda i, j: (i, 0)),
                  pl.BlockSpec((None, PEER_TE, D_MODEL), lambda i, j: (layer, jnp.minimum(j, nt - 1), 0)),
                  pl.BlockSpec((None, D_MODEL, PEER_TE), lambda i, j: (layer, 0, jnp.maximum(j - 1, 0))),
                  part, part, full, full,
                  pl.BlockSpec((PEER_HEADS, tm), lambda i, j: (0, i)),
                  _const_spec(g.shape), _const_spec(b.shape)],
        out_specs=pl.BlockSpec((tm, D_MODEL), lambda i, j: (i, 0)),
        out_shape=jax.ShapeDtypeStruct((n, D_MODEL), F32),
        scratch_shapes=[pltpu.VMEM((D_MODEL, tm), F32), pltpu.VMEM((PEER_TE, tm), F32),
                        pltpu.VMEM((PEER_TE, tm), F32), pltpu.VMEM((tm, D_MODEL), BF16),
                        pltpu.VMEM((nb, tm), F32), pltpu.VMEM((nb, tm), F32),
                        pltpu.VMEM((PEER_HEADS * SUBLANES, tm), F32)],
        compiler_params=_cparams(("arbitrary", "arbitrary")), name="peer_dense",
    )(x, u, vt, s1, e1, s2, e2, tau, g, b)


def _peer_weights(w_q, sub_keys, exp_u, exp_v):
    nl = w_q.shape[0]
    return dict(w_qt=w_q.transpose(0, 2, 1).astype(BF16),
                keys=sub_keys.reshape(nl, 2 * PEER_HEADS, N_KEYS, HALF_KEY).astype(BF16),
                u=exp_u.astype(BF16), vt=exp_v.transpose(0, 2, 1).astype(BF16))


def _peer_ln(x, w, layer, g, b, tn_route, tm):
    s1, e1, s2, e2, tau = _peer_route(x, w["w_qt"], w["keys"], layer, tn_route)
    return _peer_dense(x, w["u"], w["vt"], s1, e1, s2, e2, tau, g, b, layer, tm)


def _s5_disc_kernel(are_ref, aim_ref, ldt_ref, bre_ref, bim_ref, abre_ref, abim_ref, bbre_ref, bbim_ref):
    lr = jnp.minimum(are_ref[...], -1e-4)
    li = aim_ref[...]
    dt = jnp.exp(ldt_ref[...])
    mag = jnp.exp(lr * dt)
    ab_re = mag * jnp.cos(li * dt)
    ab_im = mag * jnp.sin(li * dt)
    den = lr * lr + li * li
    nr, ni = ab_re - 1.0, ab_im
    f_re = (nr * lr + ni * li) / den
    f_im = (ni * lr - nr * li) / den
    br, bi = bre_ref[...], bim_ref[...]
    abre_ref[...] = ab_re
    abim_ref[...] = ab_im
    bbre_ref[...] = f_re * br - f_im * bi
    bbim_ref[...] = f_re * bi + f_im * br


def _s5_weights(w_in, a_re, a_im, log_dt, b_re, b_im, c_re, c_im, d_skip, w_glu, w_o):
    rep = lambda a: jnp.repeat(a, GROUP_SIZE, axis=0)
    bt = lambda b: b.transpose(0, 2, 1).reshape(D_MODEL, STATE)
    args = (rep(a_re), rep(a_im), rep(log_dt.reshape(N_GROUPS, 1)), bt(b_re), bt(b_im))
    sd = jax.ShapeDtypeStruct((D_MODEL, STATE), F32)
    ab_re, ab_im, bb_re, bb_im = pl.pallas_call(
        _s5_disc_kernel, grid=(1,), in_specs=[_const_spec(a.shape) for a in args],
        out_specs=[_const_spec(sd.shape)] * 4, out_shape=[sd] * 4,
        compiler_params=_cparams(("arbitrary",)), name="s5_discretize")(*args)
    nblk = 4
    gl = N_GROUPS // nblk
    eye = jnp.eye(gl, dtype=F32)

    def bdiag_in(bb):
        t = bb.reshape(nblk, gl, GROUP_SIZE, STATE)
        return jnp.einsum('kgpn,gh->kgphn', t, eye).reshape(nblk, gl * GROUP_SIZE, gl * STATE).astype(BF16)

    def bdiag_out(c):
        t = c.reshape(nblk, gl, GROUP_SIZE, STATE)
        return jnp.einsum('kgpn,gh->kgnhp', t, eye).reshape(nblk, gl * STATE, gl * GROUP_SIZE)

    c_cat = jnp.concatenate([bdiag_out(c_re), -bdiag_out(c_im)], axis=1).astype(BF16)
    return dict(w_in=w_in.astype(BF16), b_re=bdiag_in(bb_re), b_im=bdiag_in(bb_im),
                a_re=ab_re[::GROUP_SIZE].reshape(1, N_STATE), a_im=ab_im[::GROUP_SIZE].reshape(1, N_STATE),
                c_cat=c_cat, d=d_skip.reshape(1, D_MODEL), w_glu=w_glu.astype(BF16), w_o=w_o.astype(BF16))


S5_LW = 2 * LANES
S5_NBLK = 4


def _cmul(ar, ai, br, bi):
    return ar * br - ai * bi, ar * bi + ai * br


def _s5_scan_kernel(seg, x_ref, win_ref, bre_ref, bim_ref, are_ref, aim_ref, ccat_ref, d_ref, *rest):
    if seg:
        s0re_ref, s0im_ref, g_ref, ore_ref, oim_ref, sre_ref, sim_ref, tab_ref = rest
    else:
        g_ref, ore_ref, oim_ref, sre_ref, sim_ref, tab_ref, car_ref = rest
    i = pl.program_id(0)
    tl = x_ref.shape[0]
    period = seg if seg else SUBLANES

    @pl.when(i == 0)
    def _():
        ar = jnp.broadcast_to(are_ref[...], (SUBLANES, N_STATE))
        ai = jnp.broadcast_to(aim_ref[...], (SUBLANES, N_STATE))
        rr = lax.broadcasted_iota(jnp.int32, (SUBLANES, N_STATE), 0) % period
        pr, pi = ar, ai
        powers = [(pr, pi)]
        for _ in range(SUBLANES - 1):
            pr, pi = _cmul(pr, pi, ar, ai)
            powers.append((pr, pi))
        for k, d in enumerate((1, 2, 4)):
            tab_ref[2 * k] = jnp.where(rr >= d, powers[d - 1][0], 0.0)
            tab_ref[2 * k + 1] = jnp.where(rr >= d, powers[d - 1][1], 0.0)
        cr = jnp.zeros((SUBLANES, N_STATE), F32)
        ci = jnp.zeros((SUBLANES, N_STATE), F32)
        for p in range(SUBLANES):
            cr = jnp.where(rr == p, powers[p][0], cr)
            ci = jnp.where(rr == p, powers[p][1], ci)
        tab_ref[6] = cr
        tab_ref[7] = ci
        if not seg:
            car_ref[...] = jnp.zeros(car_ref.shape, F32)

    u = _dot(x_ref[...].astype(BF16), win_ref[...])
    ub = u.astype(BF16)
    kin = D_MODEL // S5_NBLK
    kst = N_STATE // S5_NBLK
    for k in range(S5_NBLK):
        sre_ref[:, k * kst:(k + 1) * kst] = _dot(ub[:, k * kin:(k + 1) * kin], bre_ref[k])
        sim_ref[:, k * kst:(k + 1) * kst] = _dot(ub[:, k * kin:(k + 1) * kin], bim_ref[k])

    steps = tuple(d for d in (1, 2, 4) if d < period)

    def body(gi, carry):
        rows = pl.ds(pl.multiple_of(gi * SUBLANES, SUBLANES), SUBLANES)
        for lc in range(N_STATE // S5_LW):
            ls = slice(lc * S5_LW, (lc + 1) * S5_LW)
            xr = sre_ref[rows, ls]
            xi = sim_ref[rows, ls]
            for d in steps:
                k = (1, 2, 4).index(d)
                yr, yi = _cmul(tab_ref[2 * k, :, ls], tab_ref[2 * k + 1, :, ls],
                               pltpu.roll(xr, d, 0), pltpu.roll(xi, d, 0))
                xr, xi = xr + yr, xi + yi
            if seg:
                cr, ci = s0re_ref[rows, ls], s0im_ref[rows, ls]
            else:
                cr, ci = car_ref[0, :, ls], car_ref[1, :, ls]
            yr, yi = _cmul(tab_ref[6, :, ls], tab_ref[7, :, ls], cr, ci)
            xr, xi = xr + yr, xi + yi
            sre_ref[rows, ls] = xr
            sim_ref[rows, ls] = xi
            if not seg:
                car_ref[0, :, ls] = jnp.broadcast_to(xr[SUBLANES - 1:SUBLANES, :], (SUBLANES, S5_LW))
                car_ref[1, :, ls] = jnp.broadcast_to(xi[SUBLANES - 1:SUBLANES, :], (SUBLANES, S5_LW))
        return carry

    lax.fori_loop(0, tl // SUBLANES, body, 0)

    if seg:
        ore_ref[...] = sre_ref[...]
        oim_ref[...] = sim_ref[...]
    else:
        ore_ref[...] = car_ref[0, 0:1, :]
        oim_ref[...] = car_ref[1, 0:1, :]

    kout = D_MODEL // S5_NBLK
    ys = []
    for k in range(S5_NBLK):
        st = jnp.concatenate([sre_ref[:, k * kst:(k + 1) * kst], sim_ref[:, k * kst:(k + 1) * kst]], axis=1)
        ys.append(_dot(st.astype(BF16), ccat_ref[k]))
    y = jnp.concatenate(ys, axis=1) + d_ref[...] * u
    g_ref[...] = _gelu_tanh(y).astype(BF16)


def _s5_scan(x, w, tl, s0=None):
    n = x.shape[0]
    seg = 0 if s0 is None else 4
    row = lambda width: pl.BlockSpec((tl, width), lambda i: (i, 0))
    wargs = (w["w_in"], w["b_re"], w["b_im"], w["a_re"], w["a_im"], w["c_cat"], w["d"])
    in_specs = [row(D_MODEL)] + [_const_spec(a.shape) for a in wargs]
    args = (x,) + wargs
    scratch = [pltpu.VMEM((tl, N_STATE), F32), pltpu.VMEM((tl, N_STATE), F32),
               pltpu.VMEM((8, SUBLANES, N_STATE), F32)]
    if seg:
        in_specs += [row(N_STATE), row(N_STATE)]
        args += tuple(s0)
        st_shape = jax.ShapeDtypeStruct((n, N_STATE), F32)
        st_spec = row(N_STATE)
    else:
        scratch.append(pltpu.VMEM((2, SUBLANES, N_STATE), F32))
        st_shape = jax.ShapeDtypeStruct((1, N_STATE), F32)
        st_spec = _const_spec((1, N_STATE))
    return pl.pallas_call(
        functools.partial(_s5_scan_kernel, seg), grid=(n // tl,), in_specs=in_specs,
        out_specs=[row(D_MODEL), st_spec, st_spec],
        out_shape=[jax.ShapeDtypeStruct((n, D_MODEL), BF16), st_shape, st_shape],
        scratch_shapes=scratch,
        compiler_params=_cparams(("arbitrary",)), name="s5_scan_seg" if seg else "s5_scan",
    )(*args)


def _glu_out_kernel(gin_ref, x_ref, wglu_ref, wo_ref, g_ref, b_ref, o_ref):
    ga = _dot(gin_ref[...], wglu_ref[...])
    hid = ga[:, :D_MODEL] * jax.nn.sigmoid(ga[:, D_MODEL:])
    z = ALPHA * x_ref[...] + _dot(hid.astype(BF16), wo_ref[...])
    o_ref[...] = _layer_norm(z, g_ref[...], b_ref[...])


def _glu_out(gin, x, w_glu, w_o, g, b, tn):
    n = x.shape[0]
    row = lambda: pl.BlockSpec((tn, D_MODEL), lambda i: (i, 0))
    return pl.pallas_call(
        _glu_out_kernel, grid=(n // tn,),
        in_specs=[row(), row(), _const_spec(w_glu.shape), _const_spec(w_o.shape),
                  _const_spec(g.shape), _const_spec(b.shape)],
        out_specs=row(), out_shape=jax.ShapeDtypeStruct((n, D_MODEL), F32),
        compiler_params=_cparams(("arbitrary",)), name="s5_glu_out",
    )(gin, x, w_glu, w_o, g, b)


def _tiles(n):
    fit = lambda pref: pref if n % pref == 0 else n
    return dict(proj=fit(512), flash=fit(512), scan=fit(256), seg=fit(128), glu=fit(512),
                route=fit(256), dense=fit(512))


def kernel(x_prompt, x_sample, cache_kv_latent, cache_k_rope, state_ssm_re, state_ssm_im, page_table, mla_w_in, mla_g_q, mla_g_kv, mla_w_uq, mla_w_uk, mla_w_uv, mla_w_o, ssm_w_in, ssm_a_re, ssm_a_im, ssm_log_dt, ssm_b_re, ssm_b_im, ssm_c_re, ssm_c_im, ssm_d, ssm_w_glu, ssm_w_o, peer_w_q, peer_sub_keys, peer_u, peer_v, ln_g, ln_b):
    bp, lp, _ = x_prompt.shape
    bd, td, _ = x_sample.shape
    assert bp == 1 and td * N_HEADS == ROWS
    npr, nsm = bp * lp, bd * td
    yp = x_prompt.reshape(npr, D_MODEL)
    ys = x_sample.reshape(nsm, D_MODEL)
    lng = lambda layer, k: ln_g[layer, k].reshape(1, D_MODEL)
    lnb = lambda layer, k: ln_b[layer, k].reshape(1, D_MODEL)
    tp, ts = _tiles(npr), _tiles(nsm)
    pw = _peer_weights(peer_w_q, peer_sub_keys, peer_u, peer_v)

    outs = {}
    for layer in range(DEPTH):
        j = layer // 2
        if layer % 2 == 0:
            w = _mla_weights(mla_w_in[j], mla_g_q[j], mla_g_kv[j], mla_w_uq[j], mla_w_uk[j], mla_w_uv[j], mla_w_o[j])
            cs_p = _rope_cs(jnp.arange(lp))
            ckv_p, kr_p, qn, qr, kn, krp, v = _mla_proj(yp, cs_p, w, False, tp["proj"])
            yp = _flash_prompt(qn, qr, kn, krp, v, yp, w["w_o"], lng(layer, 0), lnb(layer, 0), tp["flash"])
            cs_s = _rope_cs(jnp.tile(PAST_LEN + jnp.arange(td), bd))
            ckv_s, kr_s, qlat, qr_s = _mla_proj(ys, cs_s, w, True, nsm)
            o_lat = _decode(qlat.reshape(bd, ROWS, KV_LORA), qr_s.reshape(bd, ROWS, LANES),
                            ckv_s.reshape(bd, td, KV_LORA), kr_s.reshape(bd, td, D_ROPE),
                            cache_kv_latent[j:j + 1], jnp.swapaxes(cache_k_rope[j:j + 1], 2, 3), page_table)
            ys = _sample_out(o_lat.reshape(nsm, N_HEADS * KV_LORA), ys, w["w_uv"], w["w_o"],
                             lng(layer, 0), lnb(layer, 0))
            outs.setdefault("p_lat", []).append(ckv_p.reshape(bp, lp, KV_LORA))
            outs.setdefault("p_kr", []).append(kr_p.reshape(bp, lp, D_ROPE))
            outs.setdefault("s_lat", []).append(ckv_s.reshape(bd, td, KV_LORA))
            outs.setdefault("s_kr", []).append(kr_s.reshape(bd, td, D_ROPE))
        else:
            w = _s5_weights(ssm_w_in[j], ssm_a_re[j], ssm_a_im[j], ssm_log_dt[j], ssm_b_re[j], ssm_b_im[j],
                            ssm_c_re[j], ssm_c_im[j], ssm_d[j], ssm_w_glu[j], ssm_w_o[j])
            g_p, sre_p, sim_p = _s5_scan(yp, w, tp["scan"])
            yp = _glu_out(g_p, yp, w["w_glu"], w["w_o"], lng(layer, 0), lnb(layer, 0), tp["glu"])
            s0 = (jnp.repeat(state_ssm_re[j].reshape(bd, N_STATE), td, axis=0),
                  jnp.repeat(state_ssm_im[j].reshape(bd, N_STATE), td, axis=0))
            g_s, sre_s, sim_s = _s5_scan(ys, w, ts["seg"], s0)
            ys = _glu_out(g_s, ys, w["w_glu"], w["w_o"], lng(layer, 0), lnb(layer, 0), ts["glu"])
            outs.setdefault("p_sre", []).append(sre_p.reshape(bp, N_GROUPS, STATE))
            outs.setdefault("p_sim", []).append(sim_p.reshape(bp, N_GROUPS, STATE))
            outs.setdefault("s_sre", []).append(sre_s[td - 1::td].reshape(bd, N_GROUPS, STATE))
            outs.setdefault("s_sim", []).append(sim_s[td - 1::td].reshape(bd, N_GROUPS, STATE))
        yp = _peer_ln(yp, pw, layer, lng(layer, 1), lnb(layer, 1), tp["route"], tp["dense"])
        ys = _peer_ln(ys, pw, layer, lng(layer, 1), lnb(layer, 1), ts["route"], ts["dense"])

    return (yp.reshape(bp, lp, D_MODEL), ys.reshape(bd, td, D_MODEL),
            jnp.stack(outs["p_lat"]), jnp.stack(outs["p_kr"]), jnp.stack(outs["p_sre"]), jnp.stack(outs["p_sim"]),
            jnp.stack(outs["s_lat"]), jnp.stack(outs["s_kr"]), jnp.stack(outs["s_sre"]), jnp.stack(outs["s_sim"]))
```
